```python
import jax, jax.numpy as jnp
from jax import lax
import numpy as np

D_MODEL = 2048
BATCH = 8
SEQ = 2048
DEPTH = 4

N_MIXERS = 4
EPS = 1e-6
CHUNK = 64
RET_HEADS = 8
RET_DK = D_MODEL // RET_HEADS
RET_DV = 2 * RET_DK
RET_IN = 2 * RET_HEADS * RET_DK + 2 * RET_HEADS * RET_DV
ROPE_BASE = 10000.0
CONV_WIDTH = 3
GLA_HEADS = 4
GLA_DK = D_MODEL // 2 // GLA_HEADS
GLA_DV = D_MODEL // GLA_HEADS
GLA_GATE_RANK = 16
GLA_GATE_TAU = 16.0
GLA_IN = 2 * GLA_HEADS * GLA_DK + 2 * GLA_HEADS * GLA_DV + GLA_GATE_RANK
POOL_WINDOWS = (2, 4, 8, 16)
POOL_GROUPS = len(POOL_WINDOWS)
POOL_GROUP = D_MODEL // POOL_GROUPS
D_FF = -(-8 * D_MODEL // (3 * 256)) * 256
N_RET = (DEPTH + 3) // N_MIXERS
N_CONV = (DEPTH + 2) // N_MIXERS
N_GLA = (DEPTH + 1) // N_MIXERS
N_POOL = DEPTH // N_MIXERS

kernel_name = "interleaved_hybrid_ret_conv_gla_pool_adaln"


def rms_norm(x, gain=None):
    xf = x.astype(jnp.float32)
    y = xf * lax.rsqrt(jnp.mean(xf * xf, axis=-1, keepdims=True) + EPS)
    if gain is not None:
        y = y * gain.astype(jnp.float32)
    return y


def rope(t, cos, sin):
    t = t.astype(jnp.float32)
    half = t.shape[-1] // 2
    t1, t2 = t[..., :half], t[..., half:]
    return jnp.concatenate([t1 * cos - t2 * sin, t2 * cos + t1 * sin], axis=-1)


def chunked_gated_linear_attention(q, k, v, log_a):
    B, H, S, dk = q.shape
    dv = v.shape[-1]
    n = S // CHUNK

    def to_chunks(t):
        t = t.astype(jnp.float32)
        return t.reshape(t.shape[0], t.shape[1], n, CHUNK, t.shape[-1]).transpose(2, 0, 1, 3, 4)

    qc, kc, vc, gc = to_chunks(q), to_chunks(k), to_chunks(v), to_chunks(log_a)
    causal = jnp.tril(jnp.ones((CHUNK, CHUNK), dtype=bool))

    def step(state, inp):
        qi, ki, vi, gi = inp
        b = jnp.cumsum(gi, axis=-2)
        b_last = b[..., -1:, :]
        q_dec = qi * jnp.exp(b)
        k_dec = ki * jnp.exp(-b)
        scores = jnp.where(causal, jnp.einsum('bhid,bhjd->bhij', q_dec, k_dec), 0.0)
        o = jnp.einsum('bhij,bhjv->bhiv', scores, vi) + jnp.einsum('bhid,bhdv->bhiv', q_dec, state)
        k_carry = ki * jnp.exp(b_last - b)
        state = jnp.exp(b_last)[..., 0, :, None] * state + jnp.einsum('bhjd,bhjv->bhdv', k_carry, vi)
        return state, o

    state0 = jnp.zeros((B, H, dk, dv), jnp.float32)
    _, o = lax.scan(step, state0, (qc, kc, vc, gc))
    return o.transpose(1, 2, 0, 3, 4).reshape(B, H, S, dv)


def split_heads(t, n_heads):
    B, S, W = t.shape
    return t.reshape(B, S, n_heads, W // n_heads).transpose(0, 2, 1, 3)


def merge_heads(t):
    B, H, S, d = t.shape
    return t.transpose(0, 2, 1, 3).reshape(B, S, H * d)


def retention_mixer(h, cos, sin, w_in, w_out):
    B, S, _ = h.shape
    qk_w, v_w = RET_HEADS * RET_DK, RET_HEADS * RET_DV
    q, k, v, g = jnp.split(h @ w_in, [qk_w, 2 * qk_w, 2 * qk_w + v_w], axis=-1)
    q = rope(split_heads(q, RET_HEADS), cos, sin)
    k = rope(split_heads(k, RET_HEADS), cos, sin) * (RET_DK ** -0.5)
    v = split_heads(v, RET_HEADS)
    log_gamma = jnp.log1p(-jnp.exp2(-5.0 - jnp.arange(RET_HEADS, dtype=jnp.float32)))
    log_a = jnp.broadcast_to(log_gamma[None, :, None, None], (1, RET_HEADS, S, 1))
    o = rms_norm(chunked_gated_linear_attention(q, k, v, log_a))
    o = merge_heads(o).astype(h.dtype)
    return ((jax.nn.silu(g) * o) @ w_out).astype(h.dtype)


def short_conv_mixer(h, w_in, conv_w, w_out):
    S = h.shape[1]
    b_gate, c_gate, u = jnp.split(h @ w_in, 3, axis=-1)
    u = c_gate * u
    up = jnp.pad(u, ((0, 0), (CONV_WIDTH - 1, 0), (0, 0)))
    y = sum(conv_w[j] * up[:, j:j + S] for j in range(CONV_WIDTH))
    return ((b_gate * y) @ w_out).astype(h.dtype)


def gla_mixer(h, w_in, w_gate_up, b_gate, w_out):
    qk_w, v_w = GLA_HEADS * GLA_DK, GLA_HEADS * GLA_DV
    q, k, v, g, z = jnp.split(h @ w_in, [qk_w, 2 * qk_w, 2 * qk_w + v_w, 2 * qk_w + 2 * v_w], axis=-1)
    log_a = jax.nn.log_sigmoid((z @ w_gate_up + b_gate).astype(jnp.float32)) / GLA_GATE_TAU
    q = split_heads(q, GLA_HEADS).astype(jnp.float32) * (GLA_DK ** -0.5)
    o = chunked_gated_linear_attention(q, split_heads(k, GLA_HEADS), split_heads(v, GLA_HEADS),
                                       split_heads(log_a, GLA_HEADS))
    o = merge_heads(rms_norm(o)).astype(h.dtype)
    return ((jax.nn.silu(g) * o) @ w_out).astype(h.dtype)


def pool_mixer(h, w_group, scale):
    B, S, D = h.shape
    hf = h.astype(jnp.float32).reshape(B, S, POOL_GROUPS, POOL_GROUP)
    cs = jnp.concatenate([jnp.zeros((B, 1, POOL_GROUPS, POOL_GROUP), jnp.float32),
                          jnp.cumsum(hf, axis=1)], axis=1)
    t = jnp.arange(S)
    pooled = []
    for gi, win in enumerate(POOL_WINDOWS):
        start = jnp.maximum(t + 1 - win, 0)
        total = cs[:, 1:, gi] - cs[:, start, gi]
        count = (t + 1 - start).astype(jnp.float32)
        pooled.append(total / count[None, :, None])
    mixed = jnp.stack(pooled, axis=2) - hf
    y = jnp.einsum('bsgp,gpq->bsgq', mixed, w_group.astype(jnp.float32)).reshape(B, S, D)
    return (y * scale).astype(h.dtype)


def swiglu(h, w_in, w_out):
    a, b = jnp.split(h @ w_in, 2, axis=-1)
    return (jax.nn.silu(a) * b) @ w_out


def _normal(key, shape, scale):
    return jax.random.normal(key, shape, jnp.float32) * scale


def setup_inputs(seed: int = 0) -> dict:
    key = jax.random.key(seed)
    ks = jax.random.split(key, 20)
    D = D_MODEL
    return {
        "x": _normal(ks[0], (BATCH, SEQ, D), 1.0),
        "c": _normal(ks[1], (BATCH, D), 1.0),
        "positions": jnp.broadcast_to(jnp.arange(SEQ, dtype=jnp.int32), (BATCH, SEQ)),
        "w_mod": _normal(ks[2], (DEPTH, D, 6 * D), 0.5 * D ** -0.5),
        "b_mod": _normal(ks[3], (DEPTH, 6 * D), 0.02),
        "norm1_g": 1.0 + _normal(ks[4], (DEPTH, D), 0.02),
        "norm2_g": 1.0 + _normal(ks[5], (DEPTH, D), 0.02),
        "ret_w_in": _normal(ks[6], (N_RET, D, RET_IN), D ** -0.5),
        "ret_w_out": _normal(ks[7], (N_RET, RET_HEADS * RET_DV, D), (RET_HEADS * RET_DV) ** -0.5),
        "conv_w_in": _normal(ks[8], (N_CONV, D, 3 * D), D ** -0.5),
        "conv_w": _normal(ks[9], (N_CONV, CONV_WIDTH, D), CONV_WIDTH ** -0.5),
        "conv_w_out": _normal(ks[10], (N_CONV, D, D), D ** -0.5),
        "gla_w_in": _normal(ks[11], (N_GLA, D, GLA_IN), D ** -0.5),
        "gla_w_gate_up": _normal(ks[12], (N_GLA, GLA_GATE_RANK, GLA_HEADS * GLA_DK), GLA_GATE_RANK ** -0.5),
        "gla_b_gate": _normal(ks[13], (N_GLA, GLA_HEADS * GLA_DK), 0.02),
        "gla_w_out": _normal(ks[14], (N_GLA, GLA_HEADS * GLA_DV, D), (GLA_HEADS * GLA_DV) ** -0.5),
        "pool_w": _normal(ks[15], (N_POOL, POOL_GROUPS, POOL_GROUP, POOL_GROUP), POOL_GROUP ** -0.5),
        "pool_scale": 1.0 + _normal(ks[16], (N_POOL, D), 0.1),
        "ffn_w_in": _normal(ks[17], (DEPTH, D, 2 * D_FF), D ** -0.5),
        "ffn_w_out": _normal(ks[18], (DEPTH, D_FF, D), D_FF ** -0.5),
        "final_g": 1.0 + _normal(ks[19], (D,), 0.02),
    }


def reference(x, c, positions, w_mod, b_mod, norm1_g, norm2_g, ret_w_in, ret_w_out,
              conv_w_in, conv_w, conv_w_out, gla_w_in, gla_w_gate_up, gla_b_gate, gla_w_out,
              pool_w, pool_scale, ffn_w_in, ffn_w_out, final_g):
    half = RET_DK // 2
    inv_freq = jnp.power(ROPE_BASE, -jnp.linspace(0.0, 1.0, half, dtype=jnp.float32))
    ang = positions.astype(jnp.float32)[:, None, :, None] * inv_freq
    cos, sin = jnp.cos(ang), jnp.sin(ang)
    c_act = jax.nn.silu(c)
    for i in range(DEPTH):
        mod = (c_act @ w_mod[i] + b_mod[i])[:, None, :]
        sh1, sc1, g1, sh2, sc2, g2 = jnp.split(mod, 6, axis=-1)
        h = (rms_norm(x, norm1_g[i]) * (1.0 + sc1) + sh1).astype(x.dtype)
        m, j = i % N_MIXERS, i // N_MIXERS
        if m == 0:
            y = retention_mixer(h, cos, sin, ret_w_in[j], ret_w_out[j])
        elif m == 1:
            y = short_conv_mixer(h, conv_w_in[j], conv_w[j], conv_w_out[j])
        elif m == 2:
            y = gla_mixer(h, gla_w_in[j], gla_w_gate_up[j], gla_b_gate[j], gla_w_out[j])
        else:
            y = pool_mixer(h, pool_w[j], pool_scale[j])
        x = (x + g1 * y).astype(x.dtype)
        h = (rms_norm(x, norm2_g[i]) * (1.0 + sc2) + sh2).astype(x.dtype)
        x = (x + g2 * swiglu(h, ffn_w_in[i], ffn_w_out[i])).astype(x.dtype)
    return rms_norm(x, final_g).astype(x.dtype)
```

```python
import functools

import jax
import jax.numpy as jnp
from jax import lax
from jax.experimental import pallas as pl
from jax.experimental.pallas import tpu as pltpu

F32 = jnp.float32
BF16 = jnp.bfloat16

EPS = 1e-6
N_MIXERS = 4
RET_HEADS = 8
ROPE_BASE = 10000.0
CONV_WIDTH = 3
GLA_HEADS = 4
GLA_GATE_RANK = 16
GLA_GATE_TAU = 16.0
GLA_CHUNK = 64
POOL_WINDOWS = (2, 4, 8, 16)

V7X_VMEM_BYTES = 64 * 1024 * 1024
V7X_LANES = 128
V7X_SUBLANES = 8
MIB = 1024 * 1024

ROWS_ROPE = 1024
ROWS_PROJ = 1024
ROWS_OUT_PROJ = 1024
ROWS_OUT_PROJ_WIDE_K = 512
ROWS_RET = 512
RET_CHUNK = 256
ROWS_GLA = 256
ROWS_CONV = 1024
ROWS_POOL = 512
ROWS_FFN = 512


def _vmem_limit(pipelined_block_bytes, scratch_bytes, temp_bytes):
    need = 2 * pipelined_block_bytes + scratch_bytes + temp_bytes + 2 * MIB
    return int(min(need, V7X_VMEM_BYTES - 6 * MIB))


def _params(semantics, vmem_bytes):
    return pltpu.CompilerParams(dimension_semantics=semantics, vmem_limit_bytes=vmem_bytes)


def _norm_mod(x, gain, scale, shift):
    ms = jnp.mean(x * x, axis=-1, keepdims=True)
    return (x * lax.rsqrt(ms + EPS)) * gain * (1.0 + scale) + shift


def _silu(x):
    return x * jax.nn.sigmoid(x)


def _head_rms(o):
    return o * lax.rsqrt(jnp.mean(o * o, axis=-1, keepdims=True) + EPS)


def _mod_kernel(c_ref, w_ref, b_ref, o_ref):
    c_act = _silu(c_ref[...]).astype(BF16)
    o_ref[...] = jnp.dot(c_act, w_ref[...].astype(BF16), preferred_element_type=F32) + b_ref[...]


def _modulation(c, w_mod, b_mod):
    depth, d, n = w_mod.shape
    b = c.shape[0]
    tn = 1024
    return pl.pallas_call(
        _mod_kernel,
        grid=(depth, n // tn),
        in_specs=[
            pl.BlockSpec((b, d), lambda l, j: (0, 0)),
            pl.BlockSpec((None, d, tn), lambda l, j: (l, 0, j)),
            pl.BlockSpec((None, 1, tn), lambda l, j: (l, 0, j)),
        ],
        out_specs=pl.BlockSpec((None, b, tn), lambda l, j: (l, 0, j)),
        out_shape=jax.ShapeDtypeStruct((depth, b, n), F32),
        compiler_params=_params(("parallel", "parallel"), _vmem_limit(d * tn * 4, 0, d * tn * 2)),
        name="adaln_mod",
    )(c, w_mod, b_mod.reshape(depth, 1, n))


def _rope_kernel(pos_ref, invf_ref, cos_ref, sin_ref):
    ang = pos_ref[...].astype(F32) * invf_ref[...]
    cos_ref[...] = jnp.cos(ang)
    sin_ref[...] = jnp.sin(ang)


def _rope_tables(positions, half):
    b, s = positions.shape
    inv_freq = jnp.power(ROPE_BASE, -jnp.linspace(0.0, 1.0, half, dtype=F32)).reshape(1, half)
    ts = min(s, ROWS_ROPE)
    spec = pl.BlockSpec((None, ts, half), lambda i, j: (i, j, 0))
    return pl.pallas_call(
        _rope_kernel,
        grid=(b, s // ts),
        in_specs=[pl.BlockSpec((None, ts, 1), lambda i, j: (i, j, 0)), pl.BlockSpec((1, half), lambda i, j: (0, 0))],
        out_specs=[spec, spec],
        out_shape=[jax.ShapeDtypeStruct((b, s, half), F32)] * 2,
        compiler_params=_params(("parallel", "parallel"), _vmem_limit(3 * ts * V7X_LANES * 4, 0, 8 * MIB)),
        name="rope_tables",
    )(positions.reshape(b, s, 1), inv_freq)


def _proj_in_kernel(x_ref, sc_ref, sh_ref, g_ref, w_ref, o_ref, h_ref):
    @pl.when(pl.program_id(2) == 0)
    def _():
        h_ref[...] = _norm_mod(x_ref[...], g_ref[...], sc_ref[...], sh_ref[...]).astype(BF16)

    o_ref[...] = jnp.dot(h_ref[...], w_ref[...], preferred_element_type=F32).astype(o_ref.dtype)


def _proj_in(x, scale, shift, gain, w, tn):
    b, s, d = x.shape
    n = w.shape[1]
    tm = min(s, ROWS_PROJ)
    vec = pl.BlockSpec((None, 1, d), lambda i, j, k: (i, 0, 0))
    return pl.pallas_call(
        _proj_in_kernel,
        grid=(b, s // tm, n // tn),
        in_specs=[
            pl.BlockSpec((None, tm, d), lambda i, j, k: (i, j, 0)),
            vec,
            vec,
            pl.BlockSpec((1, d), lambda i, j, k: (0, 0)),
            pl.BlockSpec((d, tn), lambda i, j, k: (0, k)),
        ],
        out_specs=pl.BlockSpec((None, tm, tn), lambda i, j, k: (i, j, k)),
        out_shape=jax.ShapeDtypeStruct((b, s, n), BF16),
        scratch_shapes=[pltpu.VMEM((tm, d), BF16)],
        compiler_params=_params(
            ("parallel", "parallel", "arbitrary"),
            _vmem_limit(tm * d * 4 + d * tn * 2 + tm * tn * 2, tm * d * 2, tm * d * 4 + tm * tn * 4),
        ),
        name="proj_in",
    )(x, scale, shift, gain.reshape(1, d), w)


def _out_proj_kernel(a_ref, w_ref, x_ref, gate_ref, o_ref):
    y = jnp.dot(a_ref[...], w_ref[...], preferred_element_type=F32)
    o_ref[...] = x_ref[...] + gate_ref[...] * y


def _out_proj(a, w, x, gate):
    b, s, k = a.shape
    d = w.shape[1]
    tm = min(s, ROWS_OUT_PROJ_WIDE_K if k > 2048 else ROWS_OUT_PROJ)
    tn = 1024
    return pl.pallas_call(
        _out_proj_kernel,
        grid=(b, s // tm, d // tn),
        in_specs=[
            pl.BlockSpec((None, tm, k), lambda i, j, n: (i, j, 0)),
            pl.BlockSpec((k, tn), lambda i, j, n: (0, n)),
            pl.BlockSpec((None, tm, tn), lambda i, j, n: (i, j, n)),
            pl.BlockSpec((None, 1, tn), lambda i, j, n: (i, 0, n)),
        ],
        out_specs=pl.BlockSpec((None, tm, tn), lambda i, j, n: (i, j, n)),
        out_shape=jax.ShapeDtypeStruct((b, s, d), F32),
        compiler_params=_params(
            ("parallel", "parallel", "arbitrary"),
            _vmem_limit(tm * k * 2 + k * tn * 2 + 2 * tm * tn * 4, 0, tm * tn * 4),
        ),
        name="out_proj",
    )(a, w, x, gate)


def _rope_apply(t, cos, sin):
    half = t.shape[-1] // 2
    t1, t2 = t[:, :half], t[:, half:]
    return jnp.concatenate([t1 * cos - t2 * sin, t2 * cos + t1 * sin], axis=-1)


def _ret_attn_kernel(lg_ref, q_ref, k_ref, v_ref, g_ref, cos_ref, sin_ref, o_ref, state_ref, *, chunk, k_scale):
    @pl.when(pl.program_id(2) == 0)
    def _():
        state_ref[...] = jnp.zeros_like(state_ref)

    tile = q_ref.shape[0]
    lg = lg_ref[:, 0:1]
    ii = lax.broadcasted_iota(jnp.int32, (chunk, chunk), 0)
    jj = lax.broadcasted_iota(jnp.int32, (chunk, chunk), 1)
    decay = jnp.where(ii >= jj, jnp.exp((ii - jj).astype(F32) * lg), 0.0)
    r = lax.broadcasted_iota(jnp.int32, (chunk, 1), 0).astype(F32)
    row_dec = jnp.exp((r + 1.0) * lg)
    col_dec = jnp.exp((chunk - 1.0 - r) * lg)
    chunk_dec = jnp.exp(chunk * lg)

    for c in range(tile // chunk):
        sl = pl.ds(c * chunk, chunk)
        cos = cos_ref[sl, :]
        sin = sin_ref[sl, :]
        q = _rope_apply(q_ref[sl, :].astype(F32), cos, sin)
        k = _rope_apply(k_ref[sl, :].astype(F32), cos, sin) * k_scale
        v = v_ref[sl, :]
        state = state_ref[...]
        scores = lax.dot_general(q.astype(BF16), k.astype(BF16), (((1,), (1,)), ((), ())),
                                 preferred_element_type=F32)
        o = jnp.dot((scores * decay).astype(BF16), v, preferred_element_type=F32)
        o += jnp.dot((q * row_dec).astype(BF16), state.astype(BF16), preferred_element_type=F32)
        k_carry = (k * col_dec).astype(BF16)
        state_ref[...] = chunk_dec * state + lax.dot_general(
            k_carry, v, (((0,), (0,)), ((), ())), preferred_element_type=F32)
        gate = g_ref[sl, :].astype(F32)
        o_ref[sl, :] = (_silu(gate) * _head_rms(o)).astype(o_ref.dtype)


def _ret_attn(qkvg, cos, sin, heads):
    b, s, n = qkvg.shape
    dk = n // (6 * heads)
    dv = 2 * dk
    tile = min(s, ROWS_RET)
    chunk = min(tile, RET_CHUNK)
    log_gamma = jnp.log1p(-jnp.exp2(-5.0 - jnp.arange(heads, dtype=F32)))
    lg = jnp.broadcast_to(log_gamma[:, None, None], (heads, 1, V7X_LANES))
    rope_spec = pl.BlockSpec((None, tile, dk // 2), lambda i, h, j: (i, j, 0))
    return pl.pallas_call(
        functools.partial(_ret_attn_kernel, chunk=chunk, k_scale=dk ** -0.5),
        grid=(b, heads, s // tile),
        in_specs=[
            pl.BlockSpec((None, 1, V7X_LANES), lambda i, h, j: (h, 0, 0)),
            pl.BlockSpec((None, tile, dk), lambda i, h, j: (i, j, h)),
            pl.BlockSpec((None, tile, dk), lambda i, h, j: (i, j, heads + h)),
            pl.BlockSpec((None, tile, dv), lambda i, h, j: (i, j, heads + h)),
            pl.BlockSpec((None, tile, dv), lambda i, h, j: (i, j, 2 * heads + h)),
            rope_spec,
            rope_spec,
        ],
        out_specs=pl.BlockSpec((None, tile, dv), lambda i, h, j: (i, j, h)),
        out_shape=jax.ShapeDtypeStruct((b, s, heads * dv), BF16),
        scratch_shapes=[pltpu.VMEM((dk, dv), F32)],
        compiler_params=_params(
            ("parallel", "parallel", "arbitrary"),
            _vmem_limit(tile * (2 * dk + 3 * dv) * 2 + tile * dk * 4, dk * dv * 4, 16 * MIB),
        ),
        name="retention_core",
    )(lg, qkvg, qkvg, qkvg, qkvg, cos, sin)


def _gla_attn_kernel(q_ref, k_ref, v_ref, g_ref, z_ref, wg_ref, bg_ref, o_ref, state_ref, *, chunk, q_scale):
    @pl.when(pl.program_id(2) == 0)
    def _():
        state_ref[...] = jnp.zeros_like(state_ref)

    tile = q_ref.shape[0]
    pre = jnp.dot(z_ref[...], wg_ref[...], preferred_element_type=F32) + bg_ref[...]
    log_a = (jnp.minimum(pre, 0.0) - jnp.log1p(jnp.exp(-jnp.abs(pre)))) / GLA_GATE_TAU
    ii = lax.broadcasted_iota(jnp.int32, (chunk, chunk), 0)
    jj = lax.broadcasted_iota(jnp.int32, (chunk, chunk), 1)
    causal = ii >= jj
    tril = causal.astype(F32)

    for c in range(tile // chunk):
        sl = pl.ds(c * chunk, chunk)
        b = jnp.dot(tril, log_a[c * chunk:(c + 1) * chunk, :], preferred_element_type=F32,
                    precision=lax.Precision.HIGHEST)
        b_last = b[chunk - 1:chunk, :]
        q = q_ref[sl, :].astype(F32) * q_scale
        k = k_ref[sl, :].astype(F32)
        v = v_ref[sl, :]
        q_dec = (q * jnp.exp(b)).astype(BF16)
        k_dec = (k * jnp.exp(-b)).astype(BF16)
        scores = lax.dot_general(q_dec, k_dec, (((1,), (1,)), ((), ())), preferred_element_type=F32)
        scores = jnp.where(causal, scores, 0.0).astype(BF16)
        state = state_ref[...]
        o = jnp.dot(scores, v, preferred_element_type=F32)
        o += lax.dot_general(q_dec, state.astype(BF16), (((1,), (1,)), ((), ())), preferred_element_type=F32)
        k_carry = (k * jnp.exp(b_last - b)).astype(BF16)
        state_ref[...] = jnp.exp(b_last) * state + lax.dot_general(
            v, k_carry, (((0,), (0,)), ((), ())), preferred_element_type=F32)
        gate = g_ref[sl, :].astype(F32)
        o_ref[sl, :] = (_silu(gate) * _head_rms(o)).astype(o_ref.dtype)


def _gla_attn(proj, w_gate_up, b_gate, heads):
    b, s, n = proj.shape
    dk = (n - V7X_LANES) // (6 * heads)
    dv = 2 * dk
    tile = min(s, ROWS_GLA)
    return pl.pallas_call(
        functools.partial(_gla_attn_kernel, chunk=GLA_CHUNK, q_scale=dk ** -0.5),
        grid=(b, heads, s // tile),
        in_specs=[
            pl.BlockSpec((None, tile, dk), lambda i, h, j: (i, j, h)),
            pl.BlockSpec((None, tile, dk), lambda i, h, j: (i, j, heads + h)),
            pl.BlockSpec((None, tile, dv), lambda i, h, j: (i, j, heads + h)),
            pl.BlockSpec((None, tile, dv), lambda i, h, j: (i, j, 2 * heads + h)),
            pl.BlockSpec((None, tile, V7X_LANES), lambda i, h, j: (i, j, 6 * heads * dk // V7X_LANES)),
            pl.BlockSpec((V7X_LANES, dk), lambda i, h, j: (0, h)),
            pl.BlockSpec((1, dk), lambda i, h, j: (0, h)),
        ],
        out_specs=pl.BlockSpec((None, tile, dv), lambda i, h, j: (i, j, h)),
        out_shape=jax.ShapeDtypeStruct((b, s, heads * dv), BF16),
        scratch_shapes=[pltpu.VMEM((dv, dk), F32)],
        compiler_params=_params(
            ("parallel", "parallel", "arbitrary"),
            _vmem_limit(tile * (2 * dk + 3 * dv + V7X_LANES) * 2 + V7X_LANES * dk * 2, dk * dv * 4, 16 * MIB),
        ),
        name="gla_core",
    )(proj, proj, proj, proj, proj, w_gate_up, b_gate.reshape(1, -1))


def _conv_in_kernel(x_ref, sc_ref, sh_ref, g_ref, wb_ref, wc_ref, wu_ref, cw_ref, o_ref, h_ref, carry_ref):
    j = pl.program_id(1)
    n = pl.program_id(2)
    tm, tn = o_ref.shape

    @pl.when(n == 0)
    def _():
        h_ref[...] = _norm_mod(x_ref[...], g_ref[...], sc_ref[...], sh_ref[...]).astype(BF16)

    h = h_ref[...]
    b_gate = jnp.dot(h, wb_ref[...], preferred_element_type=F32)
    c_gate = jnp.dot(h, wc_ref[...], preferred_element_type=F32)
    u = c_gate * jnp.dot(h, wu_ref[...], preferred_element_type=F32)

    prev = jnp.where(j > 0, carry_ref[n], 0.0)
    carry_ref[n] = u[tm - V7X_SUBLANES:, :]
    row = lax.broadcasted_iota(jnp.int32, (tm, 1), 0)
    u1 = jnp.where(row == 0, prev[7:8, :], pltpu.roll(u, 1, axis=0))
    u2 = jnp.where(row == 0, prev[6:7, :], jnp.where(row == 1, prev[7:8, :], pltpu.roll(u, 2, axis=0)))
    y = cw_ref[0:1, :] * u2 + cw_ref[1:2, :] * u1 + cw_ref[2:3, :] * u
    o_ref[...] = (b_gate * y).astype(o_ref.dtype)


def _conv_in(x, scale, shift, gain, w_in, conv_w):
    b, s, d = x.shape
    tm = min(s, ROWS_CONV)
    tn = 512
    nb = d // tn
    vec = pl.BlockSpec((None, 1, d), lambda i, j, n: (i, 0, 0))

    def w_spec(part):
        return pl.BlockSpec((d, tn), lambda i, j, n: (0, part * nb + n))

    return pl.pallas_call(
        _conv_in_kernel,
        grid=(b, s // tm, nb),
        in_specs=[
            pl.BlockSpec((None, tm, d), lambda i, j, n: (i, j, 0)),
            vec,
            vec,
            pl.BlockSpec((1, d), lambda i, j, n: (0, 0)),
            w_spec(0),
            w_spec(1),
            w_spec(2),
            pl.BlockSpec((CONV_WIDTH, tn), lambda i, j, n: (0, n)),
        ],
        out_specs=pl.BlockSpec((None, tm, tn), lambda i, j, n: (i, j, n)),
        out_shape=jax.ShapeDtypeStruct((b, s, d), BF16),
        scratch_shapes=[pltpu.VMEM((tm, d), BF16), pltpu.VMEM((nb, V7X_SUBLANES, tn), F32)],
        compiler_params=_params(
            ("parallel", "arbitrary", "arbitrary"),
            _vmem_limit(tm * d * 4 + 3 * d * tn * 2 + tm * tn * 2, tm * d * 2, tm * d * 4 + 8 * tm * tn * 4),
        ),
        name="conv_in",
    )(x, scale, shift, gain.reshape(1, d), w_in, w_in, w_in, conv_w)


def _pool_kernel(x_ref, sc_ref, sh_ref, g_ref, w_ref, ps_ref, gate_ref, o_ref, carry_ref):
    j = pl.program_id(1)
    tm, d = x_ref.shape
    halo = carry_ref.shape[0]
    group = d // len(POOL_WINDOWS)
    x = x_ref[...]
    h = _norm_mod(x, g_ref[...], sc_ref[...], sh_ref[...])
    prev = jnp.where(j > 0, carry_ref[...], 0.0)
    carry_ref[...] = h[tm - halo:, :]
    t = j * tm + lax.broadcasted_iota(jnp.int32, (tm, 1), 0)

    for gi, win in enumerate(POOL_WINDOWS):
        cols = slice(gi * group, (gi + 1) * group)
        hg = h[:, cols]
        ext = jnp.concatenate([prev[:, cols], hg], axis=0)
        k = 1
        while k < win:
            ext = ext + pltpu.roll(ext, k, axis=0)
            k *= 2
        count = jnp.minimum(t + 1, win).astype(F32)
        mixed = ext[halo:, :] / count - hg
        y = jnp.dot(mixed.astype(BF16), w_ref[gi], preferred_element_type=F32)
        o_ref[:, cols] = x[:, cols] + gate_ref[:, cols] * (y * ps_ref[:, cols])


def _pool_mixer(x, scale, shift, gain, pool_w, pool_scale, gate):
    b, s, d = x.shape
    groups, p, _ = pool_w.shape
    tm = min(s, ROWS_POOL)
    halo = 2 * V7X_SUBLANES
    vec = pl.BlockSpec((None, 1, d), lambda i, j: (i, 0, 0))
    row = pl.BlockSpec((1, d), lambda i, j: (0, 0))
    tile = pl.BlockSpec((None, tm, d), lambda i, j: (i, j, 0))
    return pl.pallas_call(
        _pool_kernel,
        grid=(b, s // tm),
        in_specs=[tile, vec, vec, row, pl.BlockSpec((groups, p, p), lambda i, j: (0, 0, 0)), row, vec],
        out_specs=tile,
        out_shape=jax.ShapeDtypeStruct((b, s, d), F32),
        scratch_shapes=[pltpu.VMEM((halo, d), F32)],
        compiler_params=_params(
            ("parallel", "arbitrary"),
            _vmem_limit(2 * tm * d * 4 + groups * p * p * 2, halo * d * 4, 6 * tm * d * 4),
        ),
        name="pool_mixer",
    )(x, scale, shift, gain.reshape(1, d), pool_w, pool_scale.reshape(1, d), gate)


def _ffn_kernel(x_ref, sc_ref, sh_ref, g_ref, gate_ref, wa_ref, wb_ref, wo_ref, fg_ref, o_ref, h_ref, acc_ref,
                *, final_norm):
    f = pl.program_id(2)

    @pl.when(f == 0)
    def _():
        h_ref[...] = _norm_mod(x_ref[...], g_ref[...], sc_ref[...], sh_ref[...]).astype(BF16)
        acc_ref[...] = jnp.zeros_like(acc_ref)

    h = h_ref[...]
    a = jnp.dot(h, wa_ref[...], preferred_element_type=F32)
    b = jnp.dot(h, wb_ref[...], preferred_element_type=F32)
    acc_ref[...] += jnp.dot((_silu(a) * b).astype(BF16), wo_ref[...], preferred_element_type=F32)

    @pl.when(f == pl.num_programs(2) - 1)
    def _():
        y = x_ref[...] + gate_ref[...] * acc_ref[...]
        if final_norm:
            y = _head_rms(y) * fg_ref[...]
        o_ref[...] = y


def _ffn(x, scale, shift, gain, gate, w_in, w_out, final_gain, final_norm):
    b, s, d = x.shape
    dff = w_out.shape[0]
    tm = min(s, ROWS_FFN)
    tf = 512
    nf = dff // tf
    vec = pl.BlockSpec((None, 1, d), lambda i, j, f: (i, 0, 0))
    row = pl.BlockSpec((1, d), lambda i, j, f: (0, 0))
    tile = pl.BlockSpec((None, tm, d), lambda i, j, f: (i, j, 0))
    return pl.pallas_call(
        functools.partial(_ffn_kernel, final_norm=final_norm),
        grid=(b, s // tm, nf),
        in_specs=[
            tile,
            vec,
            vec,
            row,
            vec,
            pl.BlockSpec((d, tf), lambda i, j, f: (0, f)),
            pl.BlockSpec((d, tf), lambda i, j, f: (0, nf + f)),
            pl.BlockSpec((tf, d), lambda i, j, f: (f, 0)),
            row,
        ],
        out_specs=tile,
        out_shape=jax.ShapeDtypeStruct((b, s, d), F32),
        scratch_shapes=[pltpu.VMEM((tm, d), BF16), pltpu.VMEM((tm, d), F32)],
        compiler_params=_params(
            ("parallel", "parallel", "arbitrary"),
            _vmem_limit(2 * tm * d * 4 + 3 * d * tf * 2, tm * d * 6, tm * d * 4 + 4 * tm * tf * 4),
        ),
        name="swiglu_ffn",
    )(x, scale, shift, gain.reshape(1, d), gate, w_in, w_in, w_out, final_gain.reshape(1, d))


def kernel(x, c, positions, w_mod, b_mod, norm1_g, norm2_g, ret_w_in, ret_w_out, conv_w_in, conv_w, conv_w_out,
           gla_w_in, gla_w_gate_up, gla_b_gate, gla_w_out, pool_w, pool_scale, ffn_w_in, ffn_w_out, final_g):
    b, s, d = x.shape
    depth = w_mod.shape[0]
    assert d % (len(POOL_WINDOWS) * V7X_LANES) == 0 and s % 64 == 0

    mod = _modulation(c, w_mod, b_mod).reshape(depth, b, 6, 1, d)
    cos, sin = _rope_tables(positions, d // RET_HEADS // 2)

    for i in range(depth):
        sh1, sc1, g1, sh2, sc2, g2 = (mod[i, :, p] for p in range(6))
        m, j = i % N_MIXERS, i // N_MIXERS
        if m == 0:
            qkvg = _proj_in(x, sc1, sh1, norm1_g[i], ret_w_in[j].astype(BF16), tn=1024)
            o = _ret_attn(qkvg, cos, sin, RET_HEADS)
            x = _out_proj(o, ret_w_out[j].astype(BF16), x, g1)
        elif m == 1:
            z = _conv_in(x, sc1, sh1, norm1_g[i], conv_w_in[j].astype(BF16), conv_w[j])
            x = _out_proj(z, conv_w_out[j].astype(BF16), x, g1)
        elif m == 2:
            pad = V7X_LANES - GLA_GATE_RANK
            w_in = jnp.pad(gla_w_in[j], ((0, 0), (0, pad))).astype(BF16)
            w_up = jnp.pad(gla_w_gate_up[j], ((0, pad), (0, 0))).astype(BF16)
            proj = _proj_in(x, sc1, sh1, norm1_g[i], w_in, tn=w_in.shape[1] // 7)
            o = _gla_attn(proj, w_up, gla_b_gate[j], GLA_HEADS)
            x = _out_proj(o, gla_w_out[j].astype(BF16), x, g1)
        else:
            x = _pool_mixer(x, sc1, sh1, norm1_g[i], pool_w[j].astype(BF16), pool_scale[j], g1)
        x = _ffn(x, sc2, sh2, norm2_g[i], g2, ffn_w_in[i].astype(BF16), ffn_w_out[i].astype(BF16), final_g,
                 final_norm=(i == depth - 1))
    if depth == 0:
        raise ValueError("trunk needs at least one layer")
    return x
```

```python
import functools

import jax
import jax.numpy as jnp
from jax import lax
from jax.experimental import pallas as pl
from jax.experimental.pallas import tpu as pltpu

F32 = jnp.float32
BF16 = jnp.bfloat16

EPS = 1e-6
N_MIXERS = 4
RET_HEADS = 8
ROPE_BASE = 10000.0
CONV_WIDTH = 3
GLA_HEADS = 4
GLA_GATE_RANK = 16
GLA_GATE_TAU = 16.0
GLA_CHUNK = 64
POOL_WINDOWS = (2, 4, 8, 16)

V7X_VMEM_BYTES = 64 * 1024 * 1024
V7X_LANES = 128
V7X_SUBLANES = 8
BF16_ROWS = 2 * V7X_SUBLANES
MIB = 1024 * 1024

ROWS_ROPE = 1024
ROWS_PROJ = 1024
ROWS_OUT_PROJ = 1024
ROWS_OUT_PROJ_WIDE_K = 512
ROWS_RET = 512
RET_CHUNK = 256
ROWS_GLA = 256
ROWS_CONV = 1024
ROWS_POOL = 512
ROWS_FFN = 512
NORM_UNROLL = 4


def _vmem_limit(pipelined_block_bytes, scratch_bytes, temp_bytes):
    need = 2 * pipelined_block_bytes + scratch_bytes + temp_bytes + 2 * MIB
    return int(min(need, V7X_VMEM_BYTES - 6 * MIB))


def _params(semantics, vmem_bytes):
    return pltpu.CompilerParams(dimension_semantics=semantics, vmem_limit_bytes=vmem_bytes)


def _norm_mod(x, gain, scale, shift):
    ms = jnp.mean(x * x, axis=-1, keepdims=True)
    return (x * lax.rsqrt(ms + EPS)) * gain * (1.0 + scale) + shift


def _norm_mod_rows(x_ref, g_ref, sc_ref, sh_ref, gs_ref, h_ref, copy_ref=None):
    gs_ref[...] = g_ref[...] * (1.0 + sc_ref[...])

    def body(r, carry):
        rows = pl.ds(pl.multiple_of(r * BF16_ROWS, BF16_ROWS), BF16_ROWS)
        x = x_ref[rows, :]
        ms = jnp.mean(x * x, axis=-1, keepdims=True)
        h_ref[rows, :] = ((x * lax.rsqrt(ms + EPS)) * gs_ref[...] + sh_ref[...]).astype(h_ref.dtype)
        if copy_ref is not None:
            copy_ref[rows, :] = x
        return carry

    lax.fori_loop(0, x_ref.shape[0] // BF16_ROWS, body, 0, unroll=NORM_UNROLL)


def _silu(x):
    return x * jax.nn.sigmoid(x)


def _head_rms(o):
    return o * lax.rsqrt(jnp.mean(o * o, axis=-1, keepdims=True) + EPS)


def _mod_kernel(c_ref, w_ref, b_ref, o_ref):
    c_act = _silu(c_ref[...]).astype(BF16)
    o_ref[...] = jnp.dot(c_act, w_ref[...].astype(BF16), preferred_element_type=F32) + b_ref[...]


def _modulation(c, w_mod, b_mod):
    depth, d, n = w_mod.shape
    b = c.shape[0]
    tn = 1024
    return pl.pallas_call(
        _mod_kernel,
        grid=(depth, n // tn),
        in_specs=[
            pl.BlockSpec((b, d), lambda l, j: (0, 0)),
            pl.BlockSpec((None, d, tn), lambda l, j: (l, 0, j)),
            pl.BlockSpec((None, 1, tn), lambda l, j: (l, 0, j)),
        ],
        out_specs=pl.BlockSpec((None, b, tn), lambda l, j: (l, 0, j)),
        out_shape=jax.ShapeDtypeStruct((depth, b, n), F32),
        compiler_params=_params(("parallel", "parallel"), _vmem_limit(d * tn * 4, 0, d * tn * 2)),
        name="adaln_mod",
    )(c, w_mod, b_mod.reshape(depth, 1, n))


def _rope_kernel(pos_ref, invf_ref, cos_ref, sin_ref):
    ang = pos_ref[...].astype(F32) * invf_ref[...]
    cos_ref[...] = jnp.cos(ang)
    sin_ref[...] = jnp.sin(ang)


def _rope_tables(positions, half):
    b, s = positions.shape
    inv_freq = jnp.power(ROPE_BASE, -jnp.linspace(0.0, 1.0, half, dtype=F32)).reshape(1, half)
    ts = min(s, ROWS_ROPE)
    spec = pl.BlockSpec((None, ts, half), lambda i, j: (i, j, 0))
    return pl.pallas_call(
        _rope_kernel,
        grid=(b, s // ts),
        in_specs=[pl.BlockSpec((None, ts, 1), lambda i, j: (i, j, 0)), pl.BlockSpec((1, half), lambda i, j: (0, 0))],
        out_specs=[spec, spec],
        out_shape=[jax.ShapeDtypeStruct((b, s, half), F32)] * 2,
        compiler_params=_params(("parallel", "parallel"), _vmem_limit(3 * ts * V7X_LANES * 4, 0, 8 * MIB)),
        name="rope_tables",
    )(positions.reshape(b, s, 1), inv_freq)


def _proj_in_kernel(x_ref, sc_ref, sh_ref, g_ref, w_ref, *rest, with_z):
    if with_z:
        wz_ref, o_ref, z_ref, h_ref, gs_ref = rest
    else:
        o_ref, h_ref, gs_ref = rest

    @pl.when(pl.program_id(2) == 0)
    def _():
        _norm_mod_rows(x_ref, g_ref, sc_ref, sh_ref, gs_ref, h_ref)
        if with_z:
            z_ref[...] = jnp.dot(h_ref[...], wz_ref[...], preferred_element_type=F32).astype(z_ref.dtype)

    o_ref[...] = jnp.dot(h_ref[...], w_ref[...], preferred_element_type=F32).astype(o_ref.dtype)


def _proj_in(x, scale, shift, gain, w_stack, layer, n_cols, wz=None):
    b, s, d = x.shape
    tm = min(s, ROWS_PROJ)
    tn = 1024
    with_z = wz is not None
    vec = pl.BlockSpec((None, 1, d), lambda i, j, k: (i, 0, 0))
    in_specs = [
        pl.BlockSpec((None, tm, d), lambda i, j, k: (i, j, 0)),
        vec,
        vec,
        pl.BlockSpec((1, d), lambda i, j, k: (0, 0)),
        pl.BlockSpec((None, d, tn), lambda i, j, k: (layer, 0, k)),
    ]
    out_specs = [pl.BlockSpec((None, tm, tn), lambda i, j, k: (i, j, k))]
    out_shape = [jax.ShapeDtypeStruct((b, s, n_cols), BF16)]
    args = [x, scale, shift, gain.reshape(1, d), w_stack]
    if with_z:
        in_specs.append(pl.BlockSpec((d, V7X_LANES), lambda i, j, k: (0, 0)))
        out_specs.append(pl.BlockSpec((None, tm, V7X_LANES), lambda i, j, k: (i, j, 0)))
        out_shape.append(jax.ShapeDtypeStruct((b, s, V7X_LANES), BF16))
        args.append(wz)
    outs = pl.pallas_call(
        functools.partial(_proj_in_kernel, with_z=with_z),
        grid=(b, s // tm, n_cols // tn),
        in_specs=in_specs,
        out_specs=out_specs,
        out_shape=out_shape,
        scratch_shapes=[pltpu.VMEM((tm, d), BF16), pltpu.VMEM((1, d), F32)],
        compiler_params=_params(
            ("parallel", "parallel", "arbitrary"),
            _vmem_limit(tm * d * 4 + d * tn * 2 + tm * tn * 2 + (d + tm) * V7X_LANES * 2, tm * d * 2, tm * tn * 4),
        ),
        name="proj_in",
    )(*args)
    return outs if with_z else outs[0]


def _out_proj_kernel(a_ref, w_ref, x_ref, gate_ref, o_ref):
    y = jnp.dot(a_ref[...], w_ref[...], preferred_element_type=F32)
    o_ref[...] = x_ref[...] + gate_ref[...] * y


def _out_proj(a, w_stack, layer, x, gate):
    b, s, k = a.shape
    d = w_stack.shape[2]
    tm = min(s, ROWS_OUT_PROJ_WIDE_K if k > 2048 else ROWS_OUT_PROJ)
    tn = 1024
    return pl.pallas_call(
        _out_proj_kernel,
        grid=(b, s // tm, d // tn),
        in_specs=[
            pl.BlockSpec((None, tm, k), lambda i, j, n: (i, j, 0)),
            pl.BlockSpec((None, k, tn), lambda i, j, n: (layer, 0, n)),
            pl.BlockSpec((None, tm, tn), lambda i, j, n: (i, j, n)),
            pl.BlockSpec((None, 1, tn), lambda i, j, n: (i, 0, n)),
        ],
        out_specs=pl.BlockSpec((None, tm, tn), lambda i, j, n: (i, j, n)),
        out_shape=jax.ShapeDtypeStruct((b, s, d), F32),
        compiler_params=_params(
            ("parallel", "parallel", "arbitrary"),
            _vmem_limit(tm * k * 2 + k * tn * 2 + 2 * tm * tn * 4, 0, tm * tn * 4),
        ),
        name="out_proj",
    )(a, w_stack, x, gate)


def _rope_apply(t, cos, sin):
    half = t.shape[-1] // 2
    t1, t2 = t[:, :half], t[:, half:]
    return jnp.concatenate([t1 * cos - t2 * sin, t2 * cos + t1 * sin], axis=-1)


def _ret_attn_kernel(lg_ref, q_ref, k_ref, v_ref, g_ref, cos_ref, sin_ref, o_ref, state_ref, *, chunk, k_scale):
    @pl.when(pl.program_id(2) == 0)
    def _():
        state_ref[...] = jnp.zeros_like(state_ref)

    tile = q_ref.shape[0]
    lg = lg_ref[:, 0:1]
    ii = lax.broadcasted_iota(jnp.int32, (chunk, chunk), 0)
    jj = lax.broadcasted_iota(jnp.int32, (chunk, chunk), 1)
    decay = jnp.where(ii >= jj, jnp.exp((ii - jj).astype(F32) * lg), 0.0)
    r = lax.broadcasted_iota(jnp.int32, (chunk, 1), 0).astype(F32)
    row_dec = jnp.exp((r + 1.0) * lg)
    col_dec = jnp.exp((chunk - 1.0 - r) * lg)
    chunk_dec = jnp.exp(chunk * lg)

    state = state_ref[...]
    for c in range(tile // chunk):
        sl = pl.ds(c * chunk, chunk)
        cos = cos_ref[sl, :]
        sin = sin_ref[sl, :]
        q = _rope_apply(q_ref[sl, :].astype(F32), cos, sin)
        k = _rope_apply(k_ref[sl, :].astype(F32), cos, sin) * k_scale
        v = v_ref[sl, :]
        scores = lax.dot_general(q.astype(BF16), k.astype(BF16), (((1,), (1,)), ((), ())),
                                 preferred_element_type=F32)
        o = jnp.dot((scores * decay).astype(BF16), v, preferred_element_type=F32)
        o += jnp.dot((q * row_dec).astype(BF16), state.astype(BF16), preferred_element_type=F32)
        k_carry = (k * col_dec).astype(BF16)
        state = chunk_dec * state + lax.dot_general(k_carry, v, (((0,), (0,)), ((), ())),
                                                    preferred_element_type=F32)
        gate = g_ref[sl, :].astype(F32)
        o_ref[sl, :] = (_silu(gate) * _head_rms(o)).astype(o_ref.dtype)
    state_ref[...] = state


def _ret_attn(qkvg, cos, sin, heads):
    b, s, n = qkvg.shape
    dk = n // (6 * heads)
    dv = 2 * dk
    tile = min(s, ROWS_RET)
    chunk = min(tile, RET_CHUNK)
    log_gamma = jnp.log1p(-jnp.exp2(-5.0 - jnp.arange(heads, dtype=F32)))
    lg = jnp.broadcast_to(log_gamma[:, None, None], (heads, 1, V7X_LANES))
    rope_spec = pl.BlockSpec((None, tile, dk // 2), lambda i, h, j: (i, j, 0))
    return pl.pallas_call(
        functools.partial(_ret_attn_kernel, chunk=chunk, k_scale=dk ** -0.5),
        grid=(b, heads, s // tile),
        in_specs=[
            pl.BlockSpec((None, 1, V7X_LANES), lambda i, h, j: (h, 0, 0)),
            pl.BlockSpec((None, tile, dk), lambda i, h, j: (i, j, h)),
            pl.BlockSpec((None, tile, dk), lambda i, h, j: (i, j, heads + h)),
            pl.BlockSpec((None, tile, dv), lambda i, h, j: (i, j, heads + h)),
            pl.BlockSpec((None, tile, dv), lambda i, h, j: (i, j, 2 * heads + h)),
            rope_spec,
            rope_spec,
        ],
        out_specs=pl.BlockSpec((None, tile, dv), lambda i, h, j: (i, j, h)),
        out_shape=jax.ShapeDtypeStruct((b, s, heads * dv), BF16),
        scratch_shapes=[pltpu.VMEM((dk, dv), F32)],
        compiler_params=_params(
            ("parallel", "parallel", "arbitrary"),
            _vmem_limit(tile * (2 * dk + 3 * dv) * 2 + tile * dk * 4, dk * dv * 4, 16 * MIB),
        ),
        name="retention_core",
    )(lg, qkvg, qkvg, qkvg, qkvg, cos, sin)


def _gla_attn_kernel(q_ref, k_ref, v_ref, g_ref, z_ref, wg_ref, bg_ref, o_ref, state_ref, *, chunk, q_scale):
    @pl.when(pl.program_id(1) == 0)
    def _():
        state_ref[...] = jnp.zeros_like(state_ref)

    heads, dv, dk = state_ref.shape
    tile = q_ref.shape[0]
    shift = chunk.bit_length() - 1
    pre = jnp.dot(z_ref[...], wg_ref[...], preferred_element_type=F32) + bg_ref[...]
    log_a = (jnp.minimum(pre, 0.0) - jnp.log1p(jnp.exp(-jnp.abs(pre)))) / GLA_GATE_TAU
    ii = lax.broadcasted_iota(jnp.int32, (tile, tile), 0)
    jj = lax.broadcasted_iota(jnp.int32, (tile, tile), 1)
    same_chunk = (ii >> shift) == (jj >> shift)
    causal = jnp.logical_and(same_chunk, ii >= jj)
    sums = jnp.concatenate([jnp.where(causal, 1.0, 0.0), jnp.where(same_chunk, 1.0, 0.0)], axis=0).astype(BF16)
    hi = log_a.astype(BF16)
    rest = log_a - hi.astype(F32)
    mid = rest.astype(BF16)
    lo = (rest - mid.astype(F32)).astype(BF16)
    bb = (jnp.dot(sums, hi, preferred_element_type=F32) + jnp.dot(sums, mid, preferred_element_type=F32)
          + jnp.dot(sums, lo, preferred_element_type=F32))
    b = bb[:tile]
    b_last = bb[tile:]

    k = k_ref[...].astype(F32)
    q_dec = (q_ref[...].astype(F32) * q_scale * jnp.exp(b)).astype(BF16)
    k_dec = (k * jnp.exp(-b)).astype(BF16)
    k_carry = (k * jnp.exp(b_last - b)).astype(BF16)
    chunk_dec = [jnp.exp(b_last[c * chunk:c * chunk + 1, :]) for c in range(tile // chunk)]

    for h in range(heads):
        kc = slice(h * dk, (h + 1) * dk)
        vc = slice(h * dv, (h + 1) * dv)
        v = v_ref[:, vc]
        scores = lax.dot_general(q_dec[:, kc], k_dec[:, kc], (((1,), (1,)), ((), ())), preferred_element_type=F32)
        scores = jnp.where(causal, scores, 0.0).astype(BF16)
        o = jnp.dot(scores, v, preferred_element_type=F32)
        state = state_ref[h]
        inter = []
        for c in range(tile // chunk):
            rows = slice(c * chunk, (c + 1) * chunk)
            inter.append(lax.dot_general(q_dec[rows, kc], state.astype(BF16), (((1,), (1,)), ((), ())),
                                         preferred_element_type=F32))
            update = lax.dot_general(v[rows], k_carry[rows, kc], (((0,), (0,)), ((), ())),
                                     preferred_element_type=F32)
            state = chunk_dec[c][:, kc] * state + update
        state_ref[h] = state
        o = o + jnp.concatenate(inter, axis=0)
        gate = g_ref[:, vc].astype(F32)
        o_ref[:, vc] = (_silu(gate) * _head_rms(o)).astype(o_ref.dtype)


def _gla_attn(proj, z, w_gate_up, b_gate, heads):
    b, s, n = proj.shape
    dk = n // (6 * heads)
    dv = 2 * dk
    tile = min(s, ROWS_GLA)
    return pl.pallas_call(
        functools.partial(_gla_attn_kernel, chunk=GLA_CHUNK, q_scale=dk ** -0.5),
        grid=(b, s // tile),
        in_specs=[
            pl.BlockSpec((None, tile, heads * dk), lambda i, j: (i, j, 0)),
            pl.BlockSpec((None, tile, heads * dk), lambda i, j: (i, j, 1)),
            pl.BlockSpec((None, tile, heads * dv), lambda i, j: (i, j, 1)),
            pl.BlockSpec((None, tile, heads * dv), lambda i, j: (i, j, 2)),
            pl.BlockSpec((None, tile, V7X_LANES), lambda i, j: (i, j, 0)),
            pl.BlockSpec((V7X_LANES, heads * dk), lambda i, j: (0, 0)),
            pl.BlockSpec((1, heads * dk), lambda i, j: (0, 0)),
        ],
        out_specs=pl.BlockSpec((None, tile, heads * dv), lambda i, j: (i, j, 0)),
        out_shape=jax.ShapeDtypeStruct((b, s, heads * dv), BF16),
        scratch_shapes=[pltpu.VMEM((heads, dv, dk), F32)],
        compiler_params=_params(
            ("parallel", "arbitrary"),
            _vmem_limit(tile * heads * (2 * dk + 3 * dv) * 2 + 2 * V7X_LANES * heads * dk * 2,
                        heads * dk * dv * 4, 24 * MIB),
        ),
        name="gla_core",
    )(proj, proj, proj, proj, z, w_gate_up, b_gate.reshape(1, -1))


def _conv_in_kernel(x_ref, sc_ref, sh_ref, g_ref, wb_ref, wc_ref, wu_ref, cw_ref, o_ref, h_ref, gs_ref, carry_ref):
    j = pl.program_id(1)
    n = pl.program_id(2)
    tm, tn = o_ref.shape

    @pl.when(n == 0)
    def _():
        _norm_mod_rows(x_ref, g_ref, sc_ref, sh_ref, gs_ref, h_ref)

    h = h_ref[...]
    b_gate = jnp.dot(h, wb_ref[...], preferred_element_type=F32)
    c_gate = jnp.dot(h, wc_ref[...], preferred_element_type=F32)
    u = c_gate * jnp.dot(h, wu_ref[...], preferred_element_type=F32)

    prev = jnp.where(j > 0, carry_ref[n], 0.0)
    carry_ref[n] = u[tm - V7X_SUBLANES:, :]
    row = lax.broadcasted_iota(jnp.int32, (tm, 1), 0)
    u1 = jnp.where(row == 0, prev[7:8, :], pltpu.roll(u, 1, axis=0))
    u2 = jnp.where(row == 0, prev[6:7, :], jnp.where(row == 1, prev[7:8, :], pltpu.roll(u, 2, axis=0)))
    y = cw_ref[0:1, :] * u2 + cw_ref[1:2, :] * u1 + cw_ref[2:3, :] * u
    o_ref[...] = (b_gate * y).astype(o_ref.dtype)


def _conv_in(x, scale, shift, gain, w_stack, conv_w_stack, layer):
    b, s, d = x.shape
    tm = min(s, ROWS_CONV)
    tn = 512
    nb = d // tn
    vec = pl.BlockSpec((None, 1, d), lambda i, j, n: (i, 0, 0))

    def w_spec(part):
        return pl.BlockSpec((None, d, tn), lambda i, j, n: (layer, 0, part * nb + n))

    return pl.pallas_call(
        _conv_in_kernel,
        grid=(b, s // tm, nb),
        in_specs=[
            pl.BlockSpec((None, tm, d), lambda i, j, n: (i, j, 0)),
            vec,
            vec,
            pl.BlockSpec((1, d), lambda i, j, n: (0, 0)),
            w_spec(0),
            w_spec(1),
            w_spec(2),
            pl.BlockSpec((None, CONV_WIDTH, tn), lambda i, j, n: (layer, 0, n)),
        ],
        out_specs=pl.BlockSpec((None, tm, tn), lambda i, j, n: (i, j, n)),
        out_shape=jax.ShapeDtypeStruct((b, s, d), BF16),
        scratch_shapes=[pltpu.VMEM((tm, d), BF16), pltpu.VMEM((1, d), F32),
                        pltpu.VMEM((nb, V7X_SUBLANES, tn), F32)],
        compiler_params=_params(
            ("parallel", "arbitrary", "arbitrary"),
            _vmem_limit(tm * d * 4 + 3 * d * tn * 2 + tm * tn * 2, tm * d * 2, 8 * tm * tn * 4),
        ),
        name="conv_in",
    )(x, scale, shift, gain.reshape(1, d), w_stack, w_stack, w_stack, conv_w_stack)


def _pool_kernel(x_ref, sc_ref, sh_ref, g_ref, w_ref, ps_ref, gate_ref, o_ref, carry_ref):
    j = pl.program_id(1)
    tm, d = x_ref.shape
    halo = carry_ref.shape[0]
    group = d // len(POOL_WINDOWS)
    x = x_ref[...]
    h = _norm_mod(x, g_ref[...], sc_ref[...], sh_ref[...])
    prev = jnp.where(j > 0, carry_ref[...], 0.0)
    carry_ref[...] = h[tm - halo:, :]
    t = j * tm + lax.broadcasted_iota(jnp.int32, (tm, 1), 0)

    for gi, win in enumerate(POOL_WINDOWS):
        cols = slice(gi * group, (gi + 1) * group)
        hg = h[:, cols]
        ext = jnp.concatenate([prev[:, cols], hg], axis=0)
        k = 1
        while k < win:
            ext = ext + pltpu.roll(ext, k, axis=0)
            k *= 2
        count = jnp.minimum(t + 1, win).astype(F32)
        mixed = ext[halo:, :] / count - hg
        y = jnp.dot(mixed.astype(BF16), w_ref[gi], preferred_element_type=F32)
        o_ref[:, cols] = x[:, cols] + gate_ref[:, cols] * (y * ps_ref[:, cols])


def _pool_mixer(x, scale, shift, gain, pool_w, pool_scale, gate):
    b, s, d = x.shape
    groups, p, _ = pool_w.shape
    tm = min(s, ROWS_POOL)
    halo = 2 * V7X_SUBLANES
    vec = pl.BlockSpec((None, 1, d), lambda i, j: (i, 0, 0))
    row = pl.BlockSpec((1, d), lambda i, j: (0, 0))
    tile = pl.BlockSpec((None, tm, d), lambda i, j: (i, j, 0))
    return pl.pallas_call(
        _pool_kernel,
        grid=(b, s // tm),
        in_specs=[tile, vec, vec, row, pl.BlockSpec((groups, p, p), lambda i, j: (0, 0, 0)), row, vec],
        out_specs=tile,
        out_shape=jax.ShapeDtypeStruct((b, s, d), F32),
        scratch_shapes=[pltpu.VMEM((halo, d), F32)],
        compiler_params=_params(
            ("parallel", "arbitrary"),
            _vmem_limit(2 * tm * d * 4 + groups * p * p * 2, halo * d * 4, 6 * tm * d * 4),
        ),
        name="pool_mixer",
    )(x, scale, shift, gain.reshape(1, d), pool_w, pool_scale.reshape(1, d), gate)


def _ffn_kernel(x_ref, sc_ref, sh_ref, g_ref, gate_ref, wa_ref, wb_ref, wo_ref, fg_ref, o_ref, h_ref, gs_ref,
                *, final_norm):
    f = pl.program_id(2)

    @pl.when(f == 0)
    def _():
        _norm_mod_rows(x_ref, g_ref, sc_ref, sh_ref, gs_ref, h_ref, copy_ref=o_ref)

    h = h_ref[...]
    a = jnp.dot(h, wa_ref[...], preferred_element_type=F32)
    b = jnp.dot(h, wb_ref[...], preferred_element_type=F32)
    o_ref[...] += gate_ref[...] * jnp.dot((_silu(a) * b).astype(BF16), wo_ref[...], preferred_element_type=F32)

    if final_norm:
        @pl.when(f == pl.num_programs(2) - 1)
        def _():
            def body(r, carry):
                rows = pl.ds(pl.multiple_of(r * BF16_ROWS, BF16_ROWS), BF16_ROWS)
                o_ref[rows, :] = _head_rms(o_ref[rows, :]) * fg_ref[...]
                return carry

            lax.fori_loop(0, o_ref.shape[0] // BF16_ROWS, body, 0, unroll=NORM_UNROLL)


def _ffn(x, scale, shift, gain, gate, w_in_stack, w_out_stack, layer, final_gain, final_norm):
    b, s, d = x.shape
    dff = w_out_stack.shape[1]
    tm = min(s, ROWS_FFN)
    tf = 512
    nf = dff // tf
    vec = pl.BlockSpec((None, 1, d), lambda i, j, f: (i, 0, 0))
    row = pl.BlockSpec((1, d), lambda i, j, f: (0, 0))
    tile = pl.BlockSpec((None, tm, d), lambda i, j, f: (i, j, 0))
    return pl.pallas_call(
        functools.partial(_ffn_kernel, final_norm=final_norm),
        grid=(b, s // tm, nf),
        in_specs=[
            tile,
            vec,
            vec,
            row,
            vec,
            pl.BlockSpec((None, d, tf), lambda i, j, f: (layer, 0, f)),
            pl.BlockSpec((None, d, tf), lambda i, j, f: (layer, 0, nf + f)),
            pl.BlockSpec((None, tf, d), lambda i, j, f: (layer, f, 0)),
            row,
        ],
        out_specs=tile,
        out_shape=jax.ShapeDtypeStruct((b, s, d), F32),
        scratch_shapes=[pltpu.VMEM((tm, d), BF16), pltpu.VMEM((1, d), F32)],
        compiler_params=_params(
            ("parallel", "parallel", "arbitrary"),
            _vmem_limit(2 * tm * d * 4 + 3 * d * tf * 2, tm * d * 2, tm * d * 4 + 4 * tm * tf * 4),
        ),
        name="swiglu_ffn",
    )(x, scale, shift, gain.reshape(1, d), gate, w_in_stack, w_in_stack, w_out_stack, final_gain.reshape(1, d))


def kernel(x, c, positions, w_mod, b_mod, norm1_g, norm2_g, ret_w_in, ret_w_out, conv_w_in, conv_w, conv_w_out,
           gla_w_in, gla_w_gate_up, gla_b_gate, gla_w_out, pool_w, pool_scale, ffn_w_in, ffn_w_out, final_g):
    b, s, d = x.shape
    depth = w_mod.shape[0]
    assert depth >= 1 and d % (len(POOL_WINDOWS) * V7X_LANES) == 0 and s % GLA_CHUNK == 0

    mod = _modulation(c, w_mod, b_mod).reshape(depth, b, 6, 1, d)
    cos, sin = _rope_tables(positions, d // RET_HEADS // 2)

    ret_w_in, ret_w_out, conv_w_in, conv_w_out, gla_w_in, gla_w_out, pool_w, ffn_w_in, ffn_w_out = (
        w.astype(BF16) for w in (ret_w_in, ret_w_out, conv_w_in, conv_w_out, gla_w_in, gla_w_out, pool_w,
                                 ffn_w_in, ffn_w_out))
    gla_cols = gla_w_in.shape[2] - GLA_GATE_RANK
    lane_pad = V7X_LANES - GLA_GATE_RANK

    for i in range(depth):
        sh1, sc1, g1, sh2, sc2, g2 = (mod[i, :, p] for p in range(6))
        m, j = i % N_MIXERS, i // N_MIXERS
        if m == 0:
            qkvg = _proj_in(x, sc1, sh1, norm1_g[i], ret_w_in, j, ret_w_in.shape[2])
            o = _ret_attn(qkvg, cos, sin, RET_HEADS)
            x = _out_proj(o, ret_w_out, j, x, g1)
        elif m == 1:
            z = _conv_in(x, sc1, sh1, norm1_g[i], conv_w_in, conv_w, j)
            x = _out_proj(z, conv_w_out, j, x, g1)
        elif m == 2:
            wz = jnp.pad(gla_w_in[j, :, gla_cols:], ((0, 0), (0, lane_pad)))
            w_up = jnp.pad(gla_w_gate_up[j], ((0, lane_pad), (0, 0))).astype(BF16)
            proj, z = _proj_in(x, sc1, sh1, norm1_g[i], gla_w_in, j, gla_cols, wz=wz)
            o = _gla_attn(proj, z, w_up, gla_b_gate[j], GLA_HEADS)
            x = _out_proj(o, gla_w_out, j, x, g1)
        else:
            x = _pool_mixer(x, sc1, sh1, norm1_g[i], pool_w[j], pool_scale[j], g1)
        x = _ffn(x, sc2, sh2, norm2_g[i], g2, ffn_w_in, ffn_w_out, i, final_g, final_norm=(i == depth - 1))
    return x
```

```python
import functools

import jax
import jax.numpy as jnp
from jax import lax
from jax.experimental import pallas as pl
from jax.experimental.pallas import tpu as pltpu

F32 = jnp.float32
BF16 = jnp.bfloat16

EPS = 1e-6
N_MIXERS = 4
RET_HEADS = 8
ROPE_BASE = 10000.0
CONV_WIDTH = 3
GLA_HEADS = 4
GLA_GATE_RANK = 16
GLA_GATE_TAU = 16.0
GLA_CHUNK = 64
POOL_WINDOWS = (2, 4, 8, 16)

V7X_VMEM_BYTES = 64 * 1024 * 1024
V7X_LANES = 128
V7X_SUBLANES = 8
BF16_ROWS = 2 * V7X_SUBLANES
MIB = 1024 * 1024

ROWS_ROPE = 1024
ROWS_PROJ = 1024
ROWS_OUT_PROJ = 512
ROWS_RET = 1024
RET_CHUNK = 256
ROWS_GLA = 256
ROWS_CONV = 1024
ROWS_POOL = 512
ROWS_FFN = 512
NORM_UNROLL = 4


def _vmem_limit(pipelined_block_bytes, scratch_bytes, temp_bytes):
    need = 2 * pipelined_block_bytes + scratch_bytes + temp_bytes + 2 * MIB
    return int(min(need, V7X_VMEM_BYTES - 6 * MIB))


def _params(semantics, vmem_bytes):
    return pltpu.CompilerParams(dimension_semantics=semantics, vmem_limit_bytes=vmem_bytes)


def _norm_mod(x, gain, scale, shift):
    ms = jnp.mean(x * x, axis=-1, keepdims=True)
    return (x * lax.rsqrt(ms + EPS)) * gain * (1.0 + scale) + shift


def _norm_mod_rows(x_ref, g_ref, sc_ref, sh_ref, gs_ref, h_ref, copy_ref=None):
    gs_ref[...] = g_ref[...] * (1.0 + sc_ref[...])

    def body(r, carry):
        rows = pl.ds(pl.multiple_of(r * BF16_ROWS, BF16_ROWS), BF16_ROWS)
        x = x_ref[rows, :]
        ms = jnp.mean(x * x, axis=-1, keepdims=True)
        h_ref[rows, :] = ((x * lax.rsqrt(ms + EPS)) * gs_ref[...] + sh_ref[...]).astype(h_ref.dtype)
        if copy_ref is not None:
            copy_ref[rows, :] = x
        return carry

    lax.fori_loop(0, x_ref.shape[0] // BF16_ROWS, body, 0, unroll=NORM_UNROLL)


def _silu(x):
    return x * jax.nn.sigmoid(x)


def _head_rms(o):
    return o * lax.rsqrt(jnp.mean(o * o, axis=-1, keepdims=True) + EPS)


def _mod_kernel(c_ref, w_ref, b_ref, o_ref):
    c_act = _silu(c_ref[...]).astype(BF16)
    o_ref[...] = jnp.dot(c_act, w_ref[...].astype(BF16), preferred_element_type=F32) + b_ref[...]


def _modulation(c, w_mod, b_mod):
    depth, d, n = w_mod.shape
    b = c.shape[0]
    tn = 1024
    return pl.pallas_call(
        _mod_kernel,
        grid=(depth, n // tn),
        in_specs=[
            pl.BlockSpec((b, d), lambda l, j: (0, 0)),
            pl.BlockSpec((None, d, tn), lambda l, j: (l, 0, j)),
            pl.BlockSpec((None, 1, tn), lambda l, j: (l, 0, j)),
        ],
        out_specs=pl.BlockSpec((None, b, tn), lambda l, j: (l, 0, j)),
        out_shape=jax.ShapeDtypeStruct((depth, b, n), F32),
        compiler_params=_params(("parallel", "parallel"), _vmem_limit(d * tn * 4, 0, d * tn * 2)),
        name="adaln_mod",
    )(c, w_mod, b_mod.reshape(depth, 1, n))


def _rope_kernel(pos_ref, invf_ref, cos_ref, sin_ref):
    ang = pos_ref[...].astype(F32) * invf_ref[...]
    cos_ref[...] = jnp.cos(ang)
    sin_ref[...] = jnp.sin(ang)


def _rope_tables(positions, half):
    b, s = positions.shape
    inv_freq = jnp.power(ROPE_BASE, -jnp.linspace(0.0, 1.0, half, dtype=F32)).reshape(1, half)
    ts = min(s, ROWS_ROPE)
    spec = pl.BlockSpec((None, ts, half), lambda i, j: (i, j, 0))
    return pl.pallas_call(
        _rope_kernel,
        grid=(b, s // ts),
        in_specs=[pl.BlockSpec((None, ts, 1), lambda i, j: (i, j, 0)), pl.BlockSpec((1, half), lambda i, j: (0, 0))],
        out_specs=[spec, spec],
        out_shape=[jax.ShapeDtypeStruct((b, s, half), F32)] * 2,
        compiler_params=_params(("parallel", "parallel"), _vmem_limit(3 * ts * V7X_LANES * 4, 0, 8 * MIB)),
        name="rope_tables",
    )(positions.reshape(b, s, 1), inv_freq)


def _rope_apply(t, cos, sin):
    half = cos.shape[-1]
    parts = []
    for c in range(0, t.shape[-1], 2 * half):
        t1, t2 = t[:, c:c + half], t[:, c + half:c + 2 * half]
        parts += [t1 * cos - t2 * sin, t2 * cos + t1 * sin]
    return jnp.concatenate(parts, axis=-1)


def _proj_in_kernel(*refs, col_blocks, with_rope, with_z):
    x_ref, sc_ref, sh_ref, g_ref, w_ref = refs[:5]
    rest = list(refs[5:])
    cos_ref, sin_ref = (rest.pop(0), rest.pop(0)) if with_rope else (None, None)
    wz_ref = rest.pop(0) if with_z else None
    o_ref = rest.pop(0)
    z_ref = rest.pop(0) if with_z else None
    h_ref, gs_ref = rest
    k = pl.program_id(2)

    @pl.when(k == 0)
    def _():
        _norm_mod_rows(x_ref, g_ref, sc_ref, sh_ref, gs_ref, h_ref)
        if with_z:
            z_ref[...] = jnp.dot(h_ref[...], wz_ref[...], preferred_element_type=F32).astype(z_ref.dtype)

    def emit(kind, scale):
        y = jnp.dot(h_ref[...], w_ref[...], preferred_element_type=F32)
        if kind == "rope":
            y = _rope_apply(y, cos_ref[...], sin_ref[...])
        elif kind == "silu":
            y = _silu(y)
        if scale != 1.0:
            y = y * scale
        o_ref[...] = y.astype(o_ref.dtype)

    start = 0
    for n_blocks, kind, scale in col_blocks:
        pl.when(jnp.logical_and(k >= start, k < start + n_blocks))(functools.partial(emit, kind, scale))
        start += n_blocks


def _proj_in(x, scale, shift, gain, w_stack, layer, col_blocks, rope=None, wz=None):
    b, s, d = x.shape
    tm = min(s, ROWS_PROJ)
    tn = 1024
    n_cols = tn * sum(n for n, _, _ in col_blocks)
    with_z = wz is not None
    with_rope = rope is not None
    vec = pl.BlockSpec((None, 1, d), lambda i, j, k: (i, 0, 0))
    in_specs = [
        pl.BlockSpec((None, tm, d), lambda i, j, k: (i, j, 0)),
        vec,
        vec,
        pl.BlockSpec((1, d), lambda i, j, k: (0, 0)),
        pl.BlockSpec((None, d, tn), lambda i, j, k: (layer, 0, k)),
    ]
    out_specs = [pl.BlockSpec((None, tm, tn), lambda i, j, k: (i, j, k))]
    out_shape = [jax.ShapeDtypeStruct((b, s, n_cols), BF16)]
    args = [x, scale, shift, gain.reshape(1, d), w_stack]
    if with_rope:
        half = rope[0].shape[-1]
        in_specs += [pl.BlockSpec((None, tm, half), lambda i, j, k: (i, j, 0))] * 2
        args += list(rope)
    if with_z:
        in_specs.append(pl.BlockSpec((d, V7X_LANES), lambda i, j, k: (0, 0)))
        out_specs.append(pl.BlockSpec((None, tm, V7X_LANES), lambda i, j, k: (i, j, 0)))
        out_shape.append(jax.ShapeDtypeStruct((b, s, V7X_LANES), BF16))
        args.append(wz)
    outs = pl.pallas_call(
        functools.partial(_proj_in_kernel, col_blocks=col_blocks, with_rope=with_rope, with_z=with_z),
        grid=(b, s // tm, n_cols // tn),
        in_specs=in_specs,
        out_specs=out_specs,
        out_shape=out_shape,
        scratch_shapes=[pltpu.VMEM((tm, d), BF16), pltpu.VMEM((1, d), F32)],
        compiler_params=_params(
            ("parallel", "parallel", "arbitrary"),
            _vmem_limit(tm * d * 4 + d * tn * 2 + tm * tn * 2 + (d + 5 * tm) * V7X_LANES * 2, tm * d * 2,
                        2 * tm * tn * 4),
        ),
        name="proj_in",
    )(*args)
    return outs if with_z else outs[0]


def _out_proj_kernel(a_ref, w_ref, x_ref, gate_ref, o_ref):
    y = jnp.dot(a_ref[...], w_ref[...], preferred_element_type=F32)
    o_ref[...] = x_ref[...] + gate_ref[...] * y


def _out_proj(a, w_stack, layer, x, gate):
    b, s, k = a.shape
    d = w_stack.shape[2]
    tm = min(s, ROWS_OUT_PROJ)
    tile = pl.BlockSpec((None, tm, d), lambda i, j: (i, j, 0))
    return pl.pallas_call(
        _out_proj_kernel,
        grid=(b, s // tm),
        in_specs=[
            pl.BlockSpec((None, tm, k), lambda i, j: (i, j, 0)),
            pl.BlockSpec((None, k, d), lambda i, j: (layer, 0, 0), pipeline_mode=pl.Buffered(1)),
            tile,
            pl.BlockSpec((None, 1, d), lambda i, j: (i, 0, 0)),
        ],
        out_specs=tile,
        out_shape=jax.ShapeDtypeStruct((b, s, d), F32),
        compiler_params=_params(
            ("parallel", "parallel"),
            _vmem_limit(tm * k * 2 + 2 * tm * d * 4, k * d * 2, tm * d * 4),
        ),
        name="out_proj",
    )(a, w_stack, x, gate)


def _ret_attn_kernel(lg_ref, q_ref, k_ref, v_ref, g_ref, o_ref, state_ref, *, chunk):
    @pl.when(pl.program_id(2) == 0)
    def _():
        state_ref[...] = jnp.zeros_like(state_ref)

    tile = q_ref.shape[0]
    lg = lg_ref[:, 0:1]
    ii = lax.broadcasted_iota(jnp.int32, (chunk, chunk), 0)
    jj = lax.broadcasted_iota(jnp.int32, (chunk, chunk), 1)
    decay = jnp.where(ii >= jj, jnp.exp((ii - jj).astype(F32) * lg), 0.0)
    r = lax.broadcasted_iota(jnp.int32, (chunk, 1), 0).astype(F32)
    row_dec = jnp.exp((r + 1.0) * lg)
    col_dec = jnp.exp((chunk - 1.0 - r) * lg)
    chunk_dec = jnp.exp(chunk * lg)

    state = state_ref[...]
    for c in range(tile // chunk):
        sl = pl.ds(c * chunk, chunk)
        q = q_ref[sl, :]
        k = k_ref[sl, :]
        v = v_ref[sl, :]
        scores = lax.dot_general(q, k, (((1,), (1,)), ((), ())), preferred_element_type=F32)
        o = jnp.dot((scores * decay).astype(BF16), v, preferred_element_type=F32)
        o += jnp.dot((q.astype(F32) * row_dec).astype(BF16), state.astype(BF16), preferred_element_type=F32)
        k_carry = (k.astype(F32) * col_dec).astype(BF16)
        state = chunk_dec * state + lax.dot_general(k_carry, v, (((0,), (0,)), ((), ())),
                                                    preferred_element_type=F32)
        o_ref[sl, :] = (g_ref[sl, :].astype(F32) * _head_rms(o)).astype(o_ref.dtype)
    state_ref[...] = state


def _ret_attn(qkvg, heads):
    b, s, n = qkvg.shape
    dk = n // (6 * heads)
    dv = 2 * dk
    tile = min(s, ROWS_RET)
    chunk = min(tile, RET_CHUNK)
    log_gamma = jnp.log1p(-jnp.exp2(-5.0 - jnp.arange(heads, dtype=F32)))
    lg = jnp.broadcast_to(log_gamma[:, None, None], (heads, 1, V7X_LANES))
    return pl.pallas_call(
        functools.partial(_ret_attn_kernel, chunk=chunk),
        grid=(b, heads, s // tile),
        in_specs=[
            pl.BlockSpec((None, 1, V7X_LANES), lambda i, h, j: (h, 0, 0)),
            pl.BlockSpec((None, tile, dk), lambda i, h, j: (i, j, h)),
            pl.BlockSpec((None, tile, dk), lambda i, h, j: (i, j, heads + h)),
            pl.BlockSpec((None, tile, dv), lambda i, h, j: (i, j, heads + h)),
            pl.BlockSpec((None, tile, dv), lambda i, h, j: (i, j, 2 * heads + h)),
        ],
        out_specs=pl.BlockSpec((None, tile, dv), lambda i, h, j: (i, j, h)),
        out_shape=jax.ShapeDtypeStruct((b, s, heads * dv), BF16),
        scratch_shapes=[pltpu.VMEM((dk, dv), F32)],
        compiler_params=_params(
            ("parallel", "parallel", "arbitrary"),
            _vmem_limit(tile * (2 * dk + 3 * dv) * 2, dk * dv * 4, 16 * MIB),
        ),
        name="retention_core",
    )(lg, qkvg, qkvg, qkvg, qkvg)


def _gla_attn_kernel(q_ref, k_ref, v_ref, g_ref, z_ref, wg_ref, bg_ref, o_ref, state_ref, *, chunk):
    @pl.when(pl.program_id(1) == 0)
    def _():
        state_ref[...] = jnp.zeros_like(state_ref)

    heads, dv, dk = state_ref.shape
    tile = q_ref.shape[0]
    shift = chunk.bit_length() - 1
    pre = jnp.dot(z_ref[...], wg_ref[...], preferred_element_type=F32) + bg_ref[...]
    log_a = (jnp.minimum(pre, 0.0) - jnp.log1p(jnp.exp(-jnp.abs(pre)))) / GLA_GATE_TAU
    ii = lax.broadcasted_iota(jnp.int32, (tile, tile), 0)
    jj = lax.broadcasted_iota(jnp.int32, (tile, tile), 1)
    same_chunk = (ii >> shift) == (jj >> shift)
    causal = jnp.logical_and(same_chunk, ii >= jj)
    sums = jnp.concatenate([jnp.where(causal, 1.0, 0.0), jnp.where(same_chunk, 1.0, 0.0)], axis=0).astype(BF16)
    hi = log_a.astype(BF16)
    rest = log_a - hi.astype(F32)
    mid = rest.astype(BF16)
    lo = (rest - mid.astype(F32)).astype(BF16)
    bb = (jnp.dot(sums, hi, preferred_element_type=F32) + jnp.dot(sums, mid, preferred_element_type=F32)
          + jnp.dot(sums, lo, preferred_element_type=F32))
    b = bb[:tile]
    b_last = bb[tile:]

    k = k_ref[...].astype(F32)
    q_dec = (q_ref[...].astype(F32) * jnp.exp(b)).astype(BF16)
    k_dec = (k * jnp.exp(-b)).astype(BF16)
    k_carry = (k * jnp.exp(b_last - b)).astype(BF16)
    chunk_dec = [jnp.exp(b_last[c * chunk:c * chunk + 1, :]) for c in range(tile // chunk)]

    for h in range(heads):
        kc = slice(h * dk, (h + 1) * dk)
        vc = slice(h * dv, (h + 1) * dv)
        v = v_ref[:, vc]
        scores = lax.dot_general(q_dec[:, kc], k_dec[:, kc], (((1,), (1,)), ((), ())), preferred_element_type=F32)
        scores = jnp.where(causal, scores, 0.0).astype(BF16)
        o = jnp.dot(scores, v, preferred_element_type=F32)
        state = state_ref[h]
        inter = []
        for c in range(tile // chunk):
            rows = slice(c * chunk, (c + 1) * chunk)
            inter.append(lax.dot_general(q_dec[rows, kc], state.astype(BF16), (((1,), (1,)), ((), ())),
                                         preferred_element_type=F32))
            update = lax.dot_general(v[rows], k_carry[rows, kc], (((0,), (0,)), ((), ())),
                                     preferred_element_type=F32)
            state = chunk_dec[c][:, kc] * state + update
        state_ref[h] = state
        o = o + jnp.concatenate(inter, axis=0)
        o_ref[:, vc] = (g_ref[:, vc].astype(F32) * _head_rms(o)).astype(o_ref.dtype)


def _gla_attn(proj, z, w_gate_up, b_gate, heads):
    b, s, n = proj.shape
    dk = n // (6 * heads)
    dv = 2 * dk
    tile = min(s, ROWS_GLA)
    return pl.pallas_call(
        functools.partial(_gla_attn_kernel, chunk=GLA_CHUNK),
        grid=(b, s // tile),
        in_specs=[
            pl.BlockSpec((None, tile, heads * dk), lambda i, j: (i, j, 0)),
            pl.BlockSpec((None, tile, heads * dk), lambda i, j: (i, j, 1)),
            pl.BlockSpec((None, tile, heads * dv), lambda i, j: (i, j, 1)),
            pl.BlockSpec((None, tile, heads * dv), lambda i, j: (i, j, 2)),
            pl.BlockSpec((None, tile, V7X_LANES), lambda i, j: (i, j, 0)),
            pl.BlockSpec((V7X_LANES, heads * dk), lambda i, j: (0, 0)),
            pl.BlockSpec((1, heads * dk), lambda i, j: (0, 0)),
        ],
        out_specs=pl.BlockSpec((None, tile, heads * dv), lambda i, j: (i, j, 0)),
        out_shape=jax.ShapeDtypeStruct((b, s, heads * dv), BF16),
        scratch_shapes=[pltpu.VMEM((heads, dv, dk), F32)],
        compiler_params=_params(
            ("parallel", "arbitrary"),
            _vmem_limit(tile * heads * (2 * dk + 3 * dv) * 2 + 2 * V7X_LANES * heads * dk * 2,
                        heads * dk * dv * 4, 24 * MIB),
        ),
        name="gla_core",
    )(proj, proj, proj, proj, z, w_gate_up, b_gate.reshape(1, -1))


def _conv_in_kernel(x_ref, sc_ref, sh_ref, g_ref, wb_ref, wc_ref, wu_ref, cw_ref, o_ref, h_ref, gs_ref, carry_ref):
    j = pl.program_id(1)
    n = pl.program_id(2)
    tm, tn = o_ref.shape

    @pl.when(n == 0)
    def _():
        _norm_mod_rows(x_ref, g_ref, sc_ref, sh_ref, gs_ref, h_ref)

    h = h_ref[...]
    row = lax.broadcasted_iota(jnp.int32, (tm, 1), 0)
    prev_all = jnp.where(j > 0, carry_ref[n], 0.0)
    for c in range(0, tn, 256):
        cols = slice(c, c + 256)
        b_gate = jnp.dot(h, wb_ref[:, cols], preferred_element_type=F32)
        c_gate = jnp.dot(h, wc_ref[:, cols], preferred_element_type=F32)
        u = c_gate * jnp.dot(h, wu_ref[:, cols], preferred_element_type=F32)
        prev = prev_all[:, cols]
        carry_ref[n, :, cols] = u[tm - V7X_SUBLANES:, :]
        u1 = jnp.where(row == 0, prev[7:8, :], pltpu.roll(u, 1, axis=0))
        u2 = jnp.where(row == 0, prev[6:7, :], jnp.where(row == 1, prev[7:8, :], pltpu.roll(u, 2, axis=0)))
        y = cw_ref[0:1, cols] * u2 + cw_ref[1:2, cols] * u1 + cw_ref[2:3, cols] * u
        o_ref[:, cols] = (b_gate * y).astype(o_ref.dtype)


def _conv_in(x, scale, shift, gain, w_stack, conv_w_stack, layer):
    b, s, d = x.shape
    tm = min(s, ROWS_CONV)
    tn = 512
    nb = d // tn
    vec = pl.BlockSpec((None, 1, d), lambda i, j, n: (i, 0, 0))

    def w_spec(part):
        return pl.BlockSpec((None, d, tn), lambda i, j, n: (layer, 0, part * nb + n))

    return pl.pallas_call(
        _conv_in_kernel,
        grid=(b, s // tm, nb),
        in_specs=[
            pl.BlockSpec((None, tm, d), lambda i, j, n: (i, j, 0)),
            vec,
            vec,
            pl.BlockSpec((1, d), lambda i, j, n: (0, 0)),
            w_spec(0),
            w_spec(1),
            w_spec(2),
            pl.BlockSpec((None, CONV_WIDTH, tn), lambda i, j, n: (layer, 0, n)),
        ],
        out_specs=pl.BlockSpec((None, tm, tn), lambda i, j, n: (i, j, n)),
        out_shape=jax.ShapeDtypeStruct((b, s, d), BF16),
        scratch_shapes=[pltpu.VMEM((tm, d), BF16), pltpu.VMEM((1, d), F32),
                        pltpu.VMEM((nb, V7X_SUBLANES, tn), F32)],
        compiler_params=_params(
            ("parallel", "arbitrary", "arbitrary"),
            _vmem_limit(tm * d * 4 + 3 * d * tn * 2 + tm * tn * 2, tm * d * 2, 8 * tm * tn * 4),
        ),
        name="conv_in",
    )(x, scale, shift, gain.reshape(1, d), w_stack, w_stack, w_stack, conv_w_stack)


def _pool_kernel(x_ref, sc_ref, sh_ref, g_ref, w_ref, ps_ref, gate_ref, o_ref, carry_ref):
    j = pl.program_id(1)
    tm, d = x_ref.shape
    halo = carry_ref.shape[0]
    group = d // len(POOL_WINDOWS)
    x = x_ref[...]
    h = _norm_mod(x, g_ref[...], sc_ref[...], sh_ref[...])
    prev = jnp.where(j > 0, carry_ref[...], 0.0)
    carry_ref[...] = h[tm - halo:, :]
    t = j * tm + lax.broadcasted_iota(jnp.int32, (tm, 1), 0)

    for gi, win in enumerate(POOL_WINDOWS):
        cols = slice(gi * group, (gi + 1) * group)
        hg = h[:, cols]
        ext = jnp.concatenate([prev[:, cols], hg], axis=0)
        k = 1
        while k < win:
            ext = ext + pltpu.roll(ext, k, axis=0)
            k *= 2
        count = jnp.minimum(t + 1, win).astype(F32)
        mixed = ext[halo:, :] / count - hg
        y = jnp.dot(mixed.astype(BF16), w_ref[gi], preferred_element_type=F32)
        o_ref[:, cols] = x[:, cols] + gate_ref[:, cols] * (y * ps_ref[:, cols])


def _pool_mixer(x, scale, shift, gain, pool_w, pool_scale, gate):
    b, s, d = x.shape
    groups, p, _ = pool_w.shape
    tm = min(s, ROWS_POOL)
    halo = 2 * V7X_SUBLANES
    vec = pl.BlockSpec((None, 1, d), lambda i, j: (i, 0, 0))
    row = pl.BlockSpec((1, d), lambda i, j: (0, 0))
    tile = pl.BlockSpec((None, tm, d), lambda i, j: (i, j, 0))
    return pl.pallas_call(
        _pool_kernel,
        grid=(b, s // tm),
        in_specs=[tile, vec, vec, row, pl.BlockSpec((groups, p, p), lambda i, j: (0, 0, 0)), row, vec],
        out_specs=tile,
        out_shape=jax.ShapeDtypeStruct((b, s, d), F32),
        scratch_shapes=[pltpu.VMEM((halo, d), F32)],
        compiler_params=_params(
            ("parallel", "arbitrary"),
            _vmem_limit(2 * tm * d * 4 + groups * p * p * 2, halo * d * 4, 6 * tm * d * 4),
        ),
        name="pool_mixer",
    )(x, scale, shift, gain.reshape(1, d), pool_w, pool_scale.reshape(1, d), gate)


def _ffn_kernel(x_ref, sc_ref, sh_ref, g_ref, gate_ref, wa_ref, wb_ref, wo_ref, fg_ref, o_ref, h_ref, gs_ref,
                *, final_norm):
    f = pl.program_id(2)

    @pl.when(f == 0)
    def _():
        _norm_mod_rows(x_ref, g_ref, sc_ref, sh_ref, gs_ref, h_ref, copy_ref=o_ref)

    h = h_ref[...]
    acts = []
    for c in range(0, wa_ref.shape[1], 256):
        a = jnp.dot(h, wa_ref[:, c:c + 256], preferred_element_type=F32)
        b = jnp.dot(h, wb_ref[:, c:c + 256], preferred_element_type=F32)
        acts.append((_silu(a) * b).astype(BF16))
    act = jnp.concatenate(acts, axis=-1)
    o_ref[...] += gate_ref[...] * jnp.dot(act, wo_ref[...], preferred_element_type=F32)

    if final_norm:
        @pl.when(f == pl.num_programs(2) - 1)
        def _():
            for r in range(0, o_ref.shape[0], 4 * V7X_SUBLANES):
                rows = slice(r, r + 4 * V7X_SUBLANES)
                o_ref[rows, :] = _head_rms(o_ref[rows, :]) * fg_ref[...]


def _ffn(x, scale, shift, gain, gate, w_in_stack, w_out_stack, layer, final_gain, final_norm):
    b, s, d = x.shape
    dff = w_out_stack.shape[1]
    tm = min(s, ROWS_FFN)
    tf = 512
    nf = dff // tf
    vec = pl.BlockSpec((None, 1, d), lambda i, j, f: (i, 0, 0))
    row = pl.BlockSpec((1, d), lambda i, j, f: (0, 0))
    tile = pl.BlockSpec((None, tm, d), lambda i, j, f: (i, j, 0))
    return pl.pallas_call(
        functools.partial(_ffn_kernel, final_norm=final_norm),
        grid=(b, s // tm, nf),
        in_specs=[
            tile,
            vec,
            vec,
            row,
            vec,
            pl.BlockSpec((None, d, tf), lambda i, j, f: (layer, 0, f)),
            pl.BlockSpec((None, d, tf), lambda i, j, f: (layer, 0, nf + f)),
            pl.BlockSpec((None, tf, d), lambda i, j, f: (layer, f, 0)),
            row,
        ],
        out_specs=tile,
        out_shape=jax.ShapeDtypeStruct((b, s, d), F32),
        scratch_shapes=[pltpu.VMEM((tm, d), BF16), pltpu.VMEM((1, d), F32)],
        compiler_params=_params(
            ("parallel", "parallel", "arbitrary"),
            _vmem_limit(2 * tm * d * 4 + 3 * d * tf * 2, tm * d * 2, tm * d * 4 + 4 * tm * tf * 4),
        ),
        name="swiglu_ffn",
    )(x, scale, shift, gain.reshape(1, d), gate, w_in_stack, w_in_stack, w_out_stack, final_gain.reshape(1, d))


def kernel(x, c, positions, w_mod, b_mod, norm1_g, norm2_g, ret_w_in, ret_w_out, conv_w_in, conv_w, conv_w_out,
           gla_w_in, gla_w_gate_up, gla_b_gate, gla_w_out, pool_w, pool_scale, ffn_w_in, ffn_w_out, final_g):
    b, s, d = x.shape
    depth = w_mod.shape[0]
    assert depth >= 1 and d % (len(POOL_WINDOWS) * V7X_LANES) == 0 and s % GLA_CHUNK == 0

    mod = _modulation(c, w_mod, b_mod).reshape(depth, b, 6, 1, d)
    ret_dk = d // RET_HEADS
    cos, sin = _rope_tables(positions, ret_dk // 2)

    ret_w_in, ret_w_out, conv_w_in, conv_w_out, gla_w_in, gla_w_out, pool_w, ffn_w_in, ffn_w_out = (
        w.astype(BF16) for w in (ret_w_in, ret_w_out, conv_w_in, conv_w_out, gla_w_in, gla_w_out, pool_w,
                                 ffn_w_in, ffn_w_out))
    gla_cols = gla_w_in.shape[2] - GLA_GATE_RANK
    lane_pad = V7X_LANES - GLA_GATE_RANK

    for i in range(depth):
        sh1, sc1, g1, sh2, sc2, g2 = (mod[i, :, p] for p in range(6))
        m, j = i % N_MIXERS, i // N_MIXERS
        if m == 0:
            qk_blocks = ret_w_in.shape[2] // 6 // 1024
            col_blocks = ((qk_blocks, "rope", 1.0), (qk_blocks, "rope", ret_dk ** -0.5),
                          (2 * qk_blocks, "plain", 1.0), (2 * qk_blocks, "silu", 1.0))
            qkvg = _proj_in(x, sc1, sh1, norm1_g[i], ret_w_in, j, col_blocks, rope=(cos, sin))
            o = _ret_attn(qkvg, RET_HEADS)
            x = _out_proj(o, ret_w_out, j, x, g1)
        elif m == 1:
            z = _conv_in(x, sc1, sh1, norm1_g[i], conv_w_in, conv_w, j)
            x = _out_proj(z, conv_w_out, j, x, g1)
        elif m == 2:
            wz = jnp.pad(gla_w_in[j, :, gla_cols:], ((0, 0), (0, lane_pad)))
            w_up = jnp.pad(gla_w_gate_up[j], ((0, lane_pad), (0, 0))).astype(BF16)
            qk_blocks = gla_cols // 6 // 1024
            col_blocks = ((qk_blocks, "plain", (gla_cols // 6 // GLA_HEADS) ** -0.5), (qk_blocks, "plain", 1.0),
                          (2 * qk_blocks, "plain", 1.0), (2 * qk_blocks, "silu", 1.0))
            proj, z = _proj_in(x, sc1, sh1, norm1_g[i], gla_w_in, j, col_blocks, wz=wz)
            o = _gla_attn(proj, z, w_up, gla_b_gate[j], GLA_HEADS)
            x = _out_proj(o, gla_w_out, j, x, g1)
        else:
            x = _pool_mixer(x, sc1, sh1, norm1_g[i], pool_w[j], pool_scale[j], g1)
        x = _ffn(x, sc2, sh2, norm2_g[i], g2, ffn_w_in, ffn_w_out, i, final_g, final_norm=(i == depth - 1))
    return x
```

```python
import functools

import jax
import jax.numpy as jnp
from jax import lax
from jax.experimental import pallas as pl
from jax.experimental.pallas import tpu as pltpu

F32 = jnp.float32
BF16 = jnp.bfloat16

EPS = 1e-6
N_MIXERS = 4
RET_HEADS = 8
ROPE_BASE = 10000.0
CONV_WIDTH = 3
GLA_HEADS = 4
GLA_GATE_RANK = 16
GLA_GATE_TAU = 16.0
GLA_CHUNK = 64
POOL_WINDOWS = (2, 4, 8, 16)

V7X_VMEM_BYTES = 64 * 1024 * 1024
V7X_LANES = 128
V7X_SUBLANES = 8
BF16_ROWS = 2 * V7X_SUBLANES
MIB = 1024 * 1024

ROWS_ROPE = 1024
ROWS_PROJ = 1024
ROWS_OUT_PROJ = 512
ROWS_RET = 1024
RET_CHUNK = 256
ROWS_GLA = 256
ROWS_CONV = 1024
ROWS_POOL = 512
ROWS_FFN = 1024
NORM_UNROLL = 4


def _vmem_limit(pipelined_block_bytes, scratch_bytes, temp_bytes):
    need = 2 * pipelined_block_bytes + scratch_bytes + temp_bytes + 2 * MIB
    return int(min(need, V7X_VMEM_BYTES - 6 * MIB))


def _params(semantics, vmem_bytes):
    return pltpu.CompilerParams(dimension_semantics=semantics, vmem_limit_bytes=vmem_bytes)


def _norm_mod(x, gain, scale, shift):
    ms = jnp.mean(x * x, axis=-1, keepdims=True)
    return (x * lax.rsqrt(ms + EPS)) * gain * (1.0 + scale) + shift


def _norm_mod_rows(x_ref, g_ref, sc_ref, sh_ref, gs_ref, h_ref, copy_ref=None):
    gs_ref[...] = g_ref[...] * (1.0 + sc_ref[...])

    def body(r, carry):
        rows = pl.ds(pl.multiple_of(r * BF16_ROWS, BF16_ROWS), BF16_ROWS)
        x = x_ref[rows, :]
        ms = jnp.mean(x * x, axis=-1, keepdims=True)
        h_ref[rows, :] = ((x * lax.rsqrt(ms + EPS)) * gs_ref[...] + sh_ref[...]).astype(h_ref.dtype)
        if copy_ref is not None:
            copy_ref[rows, :] = x
        return carry

    lax.fori_loop(0, x_ref.shape[0] // BF16_ROWS, body, 0, unroll=NORM_UNROLL)


def _silu(x):
    return x * jax.nn.sigmoid(x)


def _head_rms(o):
    return o * lax.rsqrt(jnp.mean(o * o, axis=-1, keepdims=True) + EPS)


def _mod_kernel(c_ref, w_ref, b_ref, o_ref):
    c_act = _silu(c_ref[...]).astype(BF16)
    o_ref[...] = jnp.dot(c_act, w_ref[...].astype(BF16), preferred_element_type=F32) + b_ref[...]


def _modulation(c, w_mod, b_mod):
    depth, d, n = w_mod.shape
    b = c.shape[0]
    tn = 1024
    return pl.pallas_call(
        _mod_kernel,
        grid=(depth, n // tn),
        in_specs=[
            pl.BlockSpec((b, d), lambda l, j: (0, 0)),
            pl.BlockSpec((None, d, tn), lambda l, j: (l, 0, j)),
            pl.BlockSpec((None, 1, tn), lambda l, j: (l, 0, j)),
        ],
        out_specs=pl.BlockSpec((None, b, tn), lambda l, j: (l, 0, j)),
        out_shape=jax.ShapeDtypeStruct((depth, b, n), F32),
        compiler_params=_params(("parallel", "parallel"), _vmem_limit(d * tn * 4, 0, d * tn * 2)),
        name="adaln_mod",
    )(c, w_mod, b_mod.reshape(depth, 1, n))


def _rope_kernel(pos_ref, invf_ref, cos_ref, sin_ref):
    ang = pos_ref[...].astype(F32) * invf_ref[...]
    cos_ref[...] = jnp.cos(ang)
    sin_ref[...] = jnp.sin(ang)


def _rope_tables(positions, half):
    b, s = positions.shape
    inv_freq = jnp.power(ROPE_BASE, -jnp.linspace(0.0, 1.0, half, dtype=F32)).reshape(1, half)
    ts = min(s, ROWS_ROPE)
    spec = pl.BlockSpec((None, ts, half), lambda i, j: (i, j, 0))
    return pl.pallas_call(
        _rope_kernel,
        grid=(b, s // ts),
        in_specs=[pl.BlockSpec((None, ts, 1), lambda i, j: (i, j, 0)), pl.BlockSpec((1, half), lambda i, j: (0, 0))],
        out_specs=[spec, spec],
        out_shape=[jax.ShapeDtypeStruct((b, s, half), F32)] * 2,
        compiler_params=_params(("parallel", "parallel"), _vmem_limit(3 * ts * V7X_LANES * 4, 0, 8 * MIB)),
        name="rope_tables",
    )(positions.reshape(b, s, 1), inv_freq)


def _rope_apply(t, cos, sin):
    half = cos.shape[-1]
    parts = []
    for c in range(0, t.shape[-1], 2 * half):
        t1, t2 = t[:, c:c + half], t[:, c + half:c + 2 * half]
        parts += [t1 * cos - t2 * sin, t2 * cos + t1 * sin]
    return jnp.concatenate(parts, axis=-1)


def _proj_in_kernel(*refs, col_blocks, with_rope, with_z):
    x_ref, sc_ref, sh_ref, g_ref, w_ref = refs[:5]
    rest = list(refs[5:])
    cos_ref, sin_ref = (rest.pop(0), rest.pop(0)) if with_rope else (None, None)
    wz_ref = rest.pop(0) if with_z else None
    o_ref = rest.pop(0)
    z_ref = rest.pop(0) if with_z else None
    h_ref, gs_ref = rest
    k = pl.program_id(2)

    @pl.when(k == 0)
    def _():
        _norm_mod_rows(x_ref, g_ref, sc_ref, sh_ref, gs_ref, h_ref)
        if with_z:
            z_ref[...] = jnp.dot(h_ref[...], wz_ref[...], preferred_element_type=F32).astype(z_ref.dtype)

    def emit(kind, scale):
        y = jnp.dot(h_ref[...], w_ref[...], preferred_element_type=F32)
        if kind == "rope":
            y = _rope_apply(y, cos_ref[...], sin_ref[...])
        elif kind == "silu":
            y = _silu(y)
        if scale != 1.0:
            y = y * scale
        o_ref[...] = y.astype(o_ref.dtype)

    start = 0
    for n_blocks, kind, scale in col_blocks:
        pl.when(jnp.logical_and(k >= start, k < start + n_blocks))(functools.partial(emit, kind, scale))
        start += n_blocks


def _proj_in(x, scale, shift, gain, w_stack, layer, col_blocks, rope=None, wz=None):
    b, s, d = x.shape
    tm = min(s, ROWS_PROJ)
    tn = 1024
    n_cols = tn * sum(n for n, _, _ in col_blocks)
    with_z = wz is not None
    with_rope = rope is not None
    vec = pl.BlockSpec((None, 1, d), lambda i, j, k: (i, 0, 0))
    in_specs = [
        pl.BlockSpec((None, tm, d), lambda i, j, k: (i, j, 0)),
        vec,
        vec,
        pl.BlockSpec((1, d), lambda i, j, k: (0, 0)),
        pl.BlockSpec((None, d, tn), lambda i, j, k: (layer, 0, k)),
    ]
    out_specs = [pl.BlockSpec((None, tm, tn), lambda i, j, k: (i, j, k))]
    out_shape = [jax.ShapeDtypeStruct((b, s, n_cols), BF16)]
    args = [x, scale, shift, gain.reshape(1, d), w_stack]
    if with_rope:
        half = rope[0].shape[-1]
        in_specs += [pl.BlockSpec((None, tm, half), lambda i, j, k: (i, j, 0))] * 2
        args += list(rope)
    if with_z:
        in_specs.append(pl.BlockSpec((d, V7X_LANES), lambda i, j, k: (0, 0)))
        out_specs.append(pl.BlockSpec((None, tm, V7X_LANES), lambda i, j, k: (i, j, 0)))
        out_shape.append(jax.ShapeDtypeStruct((b, s, V7X_LANES), BF16))
        args.append(wz)
    outs = pl.pallas_call(
        functools.partial(_proj_in_kernel, col_blocks=col_blocks, with_rope=with_rope, with_z=with_z),
        grid=(b, s // tm, n_cols // tn),
        in_specs=in_specs,
        out_specs=out_specs,
        out_shape=out_shape,
        scratch_shapes=[pltpu.VMEM((tm, d), BF16), pltpu.VMEM((1, d), F32)],
        compiler_params=_params(
            ("parallel", "parallel", "arbitrary"),
            _vmem_limit(tm * d * 4 + d * tn * 2 + tm * tn * 2 + (d + 5 * tm) * V7X_LANES * 2, tm * d * 2,
                        2 * tm * tn * 4),
        ),
        name="proj_in",
    )(*args)
    return outs if with_z else outs[0]


def _out_proj_kernel(a_ref, w_ref, x_ref, gate_ref, o_ref):
    y = jnp.dot(a_ref[...], w_ref[...], preferred_element_type=F32)
    o_ref[...] = x_ref[...] + gate_ref[...] * y


def _out_proj(a, w_stack, layer, x, gate):
    b, s, k = a.shape
    d = w_stack.shape[2]
    tm = min(s, ROWS_OUT_PROJ)
    tile = pl.BlockSpec((None, tm, d), lambda i, j: (i, j, 0))
    return pl.pallas_call(
        _out_proj_kernel,
        grid=(b, s // tm),
        in_specs=[
            pl.BlockSpec((None, tm, k), lambda i, j: (i, j, 0)),
            pl.BlockSpec((None, k, d), lambda i, j: (layer, 0, 0), pipeline_mode=pl.Buffered(1)),
            tile,
            pl.BlockSpec((None, 1, d), lambda i, j: (i, 0, 0)),
        ],
        out_specs=tile,
        out_shape=jax.ShapeDtypeStruct((b, s, d), F32),
        compiler_params=_params(
            ("parallel", "parallel"),
            _vmem_limit(tm * k * 2 + 2 * tm * d * 4, k * d * 2, tm * d * 4),
        ),
        name="out_proj",
    )(a, w_stack, x, gate)


def _ret_attn_kernel(lg_ref, q_ref, k_ref, v_ref, g_ref, o_ref, state_ref, *, chunk):
    @pl.when(pl.program_id(2) == 0)
    def _():
        state_ref[...] = jnp.zeros_like(state_ref)

    tile = q_ref.shape[0]
    lg = lg_ref[:, 0:1]
    ii = lax.broadcasted_iota(jnp.int32, (chunk, chunk), 0)
    jj = lax.broadcasted_iota(jnp.int32, (chunk, chunk), 1)
    decay = jnp.where(ii >= jj, jnp.exp((ii - jj).astype(F32) * lg), 0.0)
    r = lax.broadcasted_iota(jnp.int32, (chunk, 1), 0).astype(F32)
    row_dec = jnp.exp((r + 1.0) * lg)
    col_dec = jnp.exp((chunk - 1.0 - r) * lg)
    chunk_dec = jnp.exp(chunk * lg)

    state = state_ref[...]
    for c in range(tile // chunk):
        sl = pl.ds(c * chunk, chunk)
        q = q_ref[sl, :]
        k = k_ref[sl, :]
        v = v_ref[sl, :]
        scores = lax.dot_general(q, k, (((1,), (1,)), ((), ())), preferred_element_type=F32)
        o = jnp.dot((scores * decay).astype(BF16), v, preferred_element_type=F32)
        o += jnp.dot((q.astype(F32) * row_dec).astype(BF16), state.astype(BF16), preferred_element_type=F32)
        k_carry = (k.astype(F32) * col_dec).astype(BF16)
        state = chunk_dec * state + lax.dot_general(k_carry, v, (((0,), (0,)), ((), ())),
                                                    preferred_element_type=F32)
        o_ref[sl, :] = (g_ref[sl, :].astype(F32) * _head_rms(o)).astype(o_ref.dtype)
    state_ref[...] = state


def _ret_attn(qkvg, heads):
    b, s, n = qkvg.shape
    dk = n // (6 * heads)
    dv = 2 * dk
    tile = min(s, ROWS_RET)
    chunk = min(tile, RET_CHUNK)
    log_gamma = jnp.log1p(-jnp.exp2(-5.0 - jnp.arange(heads, dtype=F32)))
    lg = jnp.broadcast_to(log_gamma[:, None, None], (heads, 1, V7X_LANES))
    return pl.pallas_call(
        functools.partial(_ret_attn_kernel, chunk=chunk),
        grid=(b, heads, s // tile),
        in_specs=[
            pl.BlockSpec((None, 1, V7X_LANES), lambda i, h, j: (h, 0, 0)),
            pl.BlockSpec((None, tile, dk), lambda i, h, j: (i, j, h)),
            pl.BlockSpec((None, tile, dk), lambda i, h, j: (i, j, heads + h)),
            pl.BlockSpec((None, tile, dv), lambda i, h, j: (i, j, heads + h)),
            pl.BlockSpec((None, tile, dv), lambda i, h, j: (i, j, 2 * heads + h)),
        ],
        out_specs=pl.BlockSpec((None, tile, dv), lambda i, h, j: (i, j, h)),
        out_shape=jax.ShapeDtypeStruct((b, s, heads * dv), BF16),
        scratch_shapes=[pltpu.VMEM((dk, dv), F32)],
        compiler_params=_params(
            ("parallel", "parallel", "arbitrary"),
            _vmem_limit(tile * (2 * dk + 3 * dv) * 2, dk * dv * 4, 16 * MIB),
        ),
        name="retention_core",
    )(lg, qkvg, qkvg, qkvg, qkvg)


def _gla_attn_kernel(q_ref, k_ref, v_ref, g_ref, z_ref, wg_ref, bg_ref, o_ref, state_ref, *, chunk):
    @pl.when(pl.program_id(1) == 0)
    def _():
        state_ref[...] = jnp.zeros_like(state_ref)

    heads, dv, dk = state_ref.shape
    tile = q_ref.shape[0]
    shift = chunk.bit_length() - 1
    pre = jnp.dot(z_ref[...], wg_ref[...], preferred_element_type=F32) + bg_ref[...]
    log_a = (jnp.minimum(pre, 0.0) - jnp.log1p(jnp.exp(-jnp.abs(pre)))) / GLA_GATE_TAU
    ii = lax.broadcasted_iota(jnp.int32, (tile, tile), 0)
    jj = lax.broadcasted_iota(jnp.int32, (tile, tile), 1)
    same_chunk = (ii >> shift) == (jj >> shift)
    causal = jnp.logical_and(same_chunk, ii >= jj)
    sums = jnp.concatenate([jnp.where(causal, 1.0, 0.0), jnp.where(same_chunk, 1.0, 0.0)], axis=0).astype(BF16)
    hi = log_a.astype(BF16)
    rest = log_a - hi.astype(F32)
    mid = rest.astype(BF16)
    lo = (rest - mid.astype(F32)).astype(BF16)
    bb = (jnp.dot(sums, hi, preferred_element_type=F32) + jnp.dot(sums, mid, preferred_element_type=F32)
          + jnp.dot(sums, lo, preferred_element_type=F32))
    b = bb[:tile]
    b_last = bb[tile:]

    k = k_ref[...].astype(F32)
    q_dec = (q_ref[...].astype(F32) * jnp.exp(b)).astype(BF16)
    k_dec = (k * jnp.exp(-b)).astype(BF16)
    k_carry = (k * jnp.exp(b_last - b)).astype(BF16)
    chunk_dec = [jnp.exp(b_last[c * chunk:c * chunk + 1, :]) for c in range(tile // chunk)]

    for h in range(heads):
        kc = slice(h * dk, (h + 1) * dk)
        vc = slice(h * dv, (h + 1) * dv)
        v = v_ref[:, vc]
        scores = lax.dot_general(q_dec[:, kc], k_dec[:, kc], (((1,), (1,)), ((), ())), preferred_element_type=F32)
        scores = jnp.where(causal, scores, 0.0).astype(BF16)
        o = jnp.dot(scores, v, preferred_element_type=F32)
        state = state_ref[h]
        inter = []
        for c in range(tile // chunk):
            rows = slice(c * chunk, (c + 1) * chunk)
            inter.append(lax.dot_general(q_dec[rows, kc], state.astype(BF16), (((1,), (1,)), ((), ())),
                                         preferred_element_type=F32))
            update = lax.dot_general(v[rows], k_carry[rows, kc], (((0,), (0,)), ((), ())),
                                     preferred_element_type=F32)
            state = chunk_dec[c][:, kc] * state + update
        state_ref[h] = state
        o = o + jnp.concatenate(inter, axis=0)
        o_ref[:, vc] = (g_ref[:, vc].astype(F32) * _head_rms(o)).astype(o_ref.dtype)


def _gla_attn(proj, z, w_gate_up, b_gate, heads):
    b, s, n = proj.shape
    dk = n // (6 * heads)
    dv = 2 * dk
    tile = min(s, ROWS_GLA)
    return pl.pallas_call(
        functools.partial(_gla_attn_kernel, chunk=GLA_CHUNK),
        grid=(b, s // tile),
        in_specs=[
            pl.BlockSpec((None, tile, heads * dk), lambda i, j: (i, j, 0)),
            pl.BlockSpec((None, tile, heads * dk), lambda i, j: (i, j, 1)),
            pl.BlockSpec((None, tile, heads * dv), lambda i, j: (i, j, 1)),
            pl.BlockSpec((None, tile, heads * dv), lambda i, j: (i, j, 2)),
            pl.BlockSpec((None, tile, V7X_LANES), lambda i, j: (i, j, 0)),
            pl.BlockSpec((V7X_LANES, heads * dk), lambda i, j: (0, 0)),
            pl.BlockSpec((1, heads * dk), lambda i, j: (0, 0)),
        ],
        out_specs=pl.BlockSpec((None, tile, heads * dv), lambda i, j: (i, j, 0)),
        out_shape=jax.ShapeDtypeStruct((b, s, heads * dv), BF16),
        scratch_shapes=[pltpu.VMEM((heads, dv, dk), F32)],
        compiler_params=_params(
            ("parallel", "arbitrary"),
            _vmem_limit(tile * heads * (2 * dk + 3 * dv) * 2 + 2 * V7X_LANES * heads * dk * 2,
                        heads * dk * dv * 4, 24 * MIB),
        ),
        name="gla_core",
    )(proj, proj, proj, proj, z, w_gate_up, b_gate.reshape(1, -1))


def _conv_in_kernel(x_ref, sc_ref, sh_ref, g_ref, wb_ref, wc_ref, wu_ref, cw_ref, o_ref, h_ref, gs_ref, carry_ref):
    j = pl.program_id(1)
    n = pl.program_id(2)
    tm, tn = o_ref.shape

    @pl.when(n == 0)
    def _():
        _norm_mod_rows(x_ref, g_ref, sc_ref, sh_ref, gs_ref, h_ref)

    h = h_ref[...]
    row = lax.broadcasted_iota(jnp.int32, (tm, 1), 0)
    prev_all = jnp.where(j > 0, carry_ref[n], 0.0)
    for c in range(0, tn, 256):
        cols = slice(c, c + 256)
        b_gate = jnp.dot(h, wb_ref[:, cols], preferred_element_type=F32)
        c_gate = jnp.dot(h, wc_ref[:, cols], preferred_element_type=F32)
        u = c_gate * jnp.dot(h, wu_ref[:, cols], preferred_element_type=F32)
        prev = prev_all[:, cols]
        carry_ref[n, :, cols] = u[tm - V7X_SUBLANES:, :]
        u1 = jnp.where(row == 0, prev[7:8, :], pltpu.roll(u, 1, axis=0))
        u2 = jnp.where(row == 0, prev[6:7, :], jnp.where(row == 1, prev[7:8, :], pltpu.roll(u, 2, axis=0)))
        y = cw_ref[0:1, cols] * u2 + cw_ref[1:2, cols] * u1 + cw_ref[2:3, cols] * u
        o_ref[:, cols] = (b_gate * y).astype(o_ref.dtype)


def _conv_in(x, scale, shift, gain, w_stack, conv_w_stack, layer):
    b, s, d = x.shape
    tm = min(s, ROWS_CONV)
    tn = 512
    nb = d // tn
    vec = pl.BlockSpec((None, 1, d), lambda i, j, n: (i, 0, 0))

    def w_spec(part):
        return pl.BlockSpec((None, d, tn), lambda i, j, n: (layer, 0, part * nb + n))

    return pl.pallas_call(
        _conv_in_kernel,
        grid=(b, s // tm, nb),
        in_specs=[
            pl.BlockSpec((None, tm, d), lambda i, j, n: (i, j, 0)),
            vec,
            vec,
            pl.BlockSpec((1, d), lambda i, j, n: (0, 0)),
            w_spec(0),
            w_spec(1),
            w_spec(2),
            pl.BlockSpec((None, CONV_WIDTH, tn), lambda i, j, n: (layer, 0, n)),
        ],
        out_specs=pl.BlockSpec((None, tm, tn), lambda i, j, n: (i, j, n)),
        out_shape=jax.ShapeDtypeStruct((b, s, d), BF16),
        scratch_shapes=[pltpu.VMEM((tm, d), BF16), pltpu.VMEM((1, d), F32),
                        pltpu.VMEM((nb, V7X_SUBLANES, tn), F32)],
        compiler_params=_params(
            ("parallel", "arbitrary", "arbitrary"),
            _vmem_limit(tm * d * 4 + 3 * d * tn * 2 + tm * tn * 2, tm * d * 2, 8 * tm * tn * 4),
        ),
        name="conv_in",
    )(x, scale, shift, gain.reshape(1, d), w_stack, w_stack, w_stack, conv_w_stack)


def _pool_kernel(x_ref, sc_ref, sh_ref, g_ref, w_ref, ps_ref, gate_ref, o_ref, carry_ref):
    j = pl.program_id(1)
    tm, d = x_ref.shape
    halo = carry_ref.shape[0]
    group = d // len(POOL_WINDOWS)
    x = x_ref[...]
    h = _norm_mod(x, g_ref[...], sc_ref[...], sh_ref[...])
    prev = jnp.where(j > 0, carry_ref[...], 0.0)
    carry_ref[...] = h[tm - halo:, :]
    t = j * tm + lax.broadcasted_iota(jnp.int32, (tm, 1), 0)

    for gi, win in enumerate(POOL_WINDOWS):
        cols = slice(gi * group, (gi + 1) * group)
        hg = h[:, cols]
        ext = jnp.concatenate([prev[:, cols], hg], axis=0)
        k = 1
        while k < win:
            ext = ext + pltpu.roll(ext, k, axis=0)
            k *= 2
        count = jnp.minimum(t + 1, win).astype(F32)
        mixed = ext[halo:, :] / count - hg
        y = jnp.dot(mixed.astype(BF16), w_ref[gi], preferred_element_type=F32)
        o_ref[:, cols] = x[:, cols] + gate_ref[:, cols] * (y * ps_ref[:, cols])


def _pool_mixer(x, scale, shift, gain, pool_w, pool_scale, gate):
    b, s, d = x.shape
    groups, p, _ = pool_w.shape
    tm = min(s, ROWS_POOL)
    halo = 2 * V7X_SUBLANES
    vec = pl.BlockSpec((None, 1, d), lambda i, j: (i, 0, 0))
    row = pl.BlockSpec((1, d), lambda i, j: (0, 0))
    tile = pl.BlockSpec((None, tm, d), lambda i, j: (i, j, 0))
    return pl.pallas_call(
        _pool_kernel,
        grid=(b, s // tm),
        in_specs=[tile, vec, vec, row, pl.BlockSpec((groups, p, p), lambda i, j: (0, 0, 0)), row, vec],
        out_specs=tile,
        out_shape=jax.ShapeDtypeStruct((b, s, d), F32),
        scratch_shapes=[pltpu.VMEM((halo, d), F32)],
        compiler_params=_params(
            ("parallel", "arbitrary"),
            _vmem_limit(2 * tm * d * 4 + groups * p * p * 2, halo * d * 4, 6 * tm * d * 4),
        ),
        name="pool_mixer",
    )(x, scale, shift, gain.reshape(1, d), pool_w, pool_scale.reshape(1, d), gate)


def _ffn_kernel(x_ref, sc_ref, sh_ref, g_ref, gate_ref, wa_ref, wb_ref, wo_ref, fg_ref, *rest, final_norm, cast_next):
    if cast_next:
        next_in_ref, next_out_ref, o_ref, next_in_bf_ref, next_out_bf_ref, h_ref, gs_ref = rest
    else:
        o_ref, h_ref, gs_ref = rest
    f = pl.program_id(2)

    @pl.when(f == 0)
    def _():
        _norm_mod_rows(x_ref, g_ref, sc_ref, sh_ref, gs_ref, h_ref, copy_ref=o_ref)

    h = h_ref[...]
    acts = []
    for c in range(0, wa_ref.shape[1], 256):
        a = jnp.dot(h, wa_ref[:, c:c + 256], preferred_element_type=F32)
        b = jnp.dot(h, wb_ref[:, c:c + 256], preferred_element_type=F32)
        acts.append((_silu(a) * b).astype(BF16))
    act = jnp.concatenate(acts, axis=-1)
    o_ref[...] += gate_ref[...] * jnp.dot(act, wo_ref[...], preferred_element_type=F32)
    if cast_next:
        next_in_bf_ref[...] = next_in_ref[...].astype(BF16)
        next_out_bf_ref[...] = next_out_ref[...].astype(BF16)

    if final_norm:
        @pl.when(f == pl.num_programs(2) - 1)
        def _():
            for r in range(0, o_ref.shape[0], 4 * V7X_SUBLANES):
                rows = slice(r, r + 4 * V7X_SUBLANES)
                o_ref[rows, :] = _head_rms(o_ref[rows, :]) * fg_ref[...]


def _ffn(x, scale, shift, gain, gate, w_in, w_out, final_gain, final_norm, next_f32=None):
    b, s, d = x.shape
    dff = w_out.shape[0]
    tm = min(s, ROWS_FFN)
    tf = 512
    nf = dff // tf
    nj = s // tm
    vec = pl.BlockSpec((None, 1, d), lambda i, j, f: (i, 0, 0))
    row = pl.BlockSpec((1, d), lambda i, j, f: (0, 0))
    tile = pl.BlockSpec((None, tm, d), lambda i, j, f: (i, j, 0))
    in_specs = [
        tile,
        vec,
        vec,
        row,
        vec,
        pl.BlockSpec((d, tf), lambda i, j, f: (0, f)),
        pl.BlockSpec((d, tf), lambda i, j, f: (0, nf + f)),
        pl.BlockSpec((tf, d), lambda i, j, f: (f, 0)),
        row,
    ]
    out_specs = [tile]
    out_shape = [jax.ShapeDtypeStruct((b, s, d), F32)]
    args = [x, scale, shift, gain.reshape(1, d), gate, w_in, w_in, w_out, final_gain.reshape(1, d)]
    slab_bytes = 0
    if next_f32 is not None:
        nw_in, nw_out, nl = next_f32
        slabs = 2 * dff // V7X_LANES
        out_rows = dff // slabs
        assert b * nj * nf >= slabs and out_rows % BF16_ROWS == 0

        def slab(i, j, f):
            return jnp.minimum((i * nj + j) * nf + f, slabs - 1)

        in_specs += [pl.BlockSpec((None, d, V7X_LANES), lambda i, j, f: (nl, 0, slab(i, j, f))),
                     pl.BlockSpec((None, out_rows, d), lambda i, j, f: (nl, slab(i, j, f), 0))]
        out_specs += [pl.BlockSpec((d, V7X_LANES), lambda i, j, f: (0, slab(i, j, f))),
                      pl.BlockSpec((out_rows, d), lambda i, j, f: (slab(i, j, f), 0))]
        out_shape += [jax.ShapeDtypeStruct((d, 2 * dff), BF16), jax.ShapeDtypeStruct((dff, d), BF16)]
        args += [nw_in, nw_out]
        slab_bytes = (d * V7X_LANES + out_rows * d) * 6
    outs = pl.pallas_call(
        functools.partial(_ffn_kernel, final_norm=final_norm, cast_next=next_f32 is not None),
        grid=(b, nj, nf),
        in_specs=in_specs,
        out_specs=out_specs,
        out_shape=out_shape,
        scratch_shapes=[pltpu.VMEM((tm, d), BF16), pltpu.VMEM((1, d), F32)],
        compiler_params=_params(
            ("arbitrary", "arbitrary", "arbitrary"),
            _vmem_limit(2 * tm * d * 4 + 3 * d * tf * 2 + slab_bytes, tm * d * 2, 4 * tm * 256 * 4),
        ),
        name="swiglu_ffn",
    )(*args)
    return outs if next_f32 is not None else outs[0]


def kernel(x, c, positions, w_mod, b_mod, norm1_g, norm2_g, ret_w_in, ret_w_out, conv_w_in, conv_w, conv_w_out,
           gla_w_in, gla_w_gate_up, gla_b_gate, gla_w_out, pool_w, pool_scale, ffn_w_in, ffn_w_out, final_g):
    b, s, d = x.shape
    depth = w_mod.shape[0]
    assert depth >= 1 and d % (len(POOL_WINDOWS) * V7X_LANES) == 0 and s % GLA_CHUNK == 0

    mod = _modulation(c, w_mod, b_mod).reshape(depth, b, 6, 1, d)
    ret_dk = d // RET_HEADS
    cos, sin = _rope_tables(positions, ret_dk // 2)

    ret_w_in, ret_w_out, conv_w_in, conv_w_out, gla_w_in, gla_w_out, pool_w = (
        w.astype(BF16) for w in (ret_w_in, ret_w_out, conv_w_in, conv_w_out, gla_w_in, gla_w_out, pool_w))
    ffn_in, ffn_out = ffn_w_in[0].astype(BF16), ffn_w_out[0].astype(BF16)
    gla_cols = gla_w_in.shape[2] - GLA_GATE_RANK
    lane_pad = V7X_LANES - GLA_GATE_RANK

    for i in range(depth):
        sh1, sc1, g1, sh2, sc2, g2 = (mod[i, :, p] for p in range(6))
        m, j = i % N_MIXERS, i // N_MIXERS
        if m == 0:
            qk_blocks = ret_w_in.shape[2] // 6 // 1024
            col_blocks = ((qk_blocks, "rope", 1.0), (qk_blocks, "rope", ret_dk ** -0.5),
                          (2 * qk_blocks, "plain", 1.0), (2 * qk_blocks, "silu", 1.0))
            qkvg = _proj_in(x, sc1, sh1, norm1_g[i], ret_w_in, j, col_blocks, rope=(cos, sin))
            o = _ret_attn(qkvg, RET_HEADS)
            x = _out_proj(o, ret_w_out, j, x, g1)
        elif m == 1:
            z = _conv_in(x, sc1, sh1, norm1_g[i], conv_w_in, conv_w, j)
            x = _out_proj(z, conv_w_out, j, x, g1)
        elif m == 2:
            wz = jnp.pad(gla_w_in[j, :, gla_cols:], ((0, 0), (0, lane_pad)))
            w_up = jnp.pad(gla_w_gate_up[j], ((0, lane_pad), (0, 0))).astype(BF16)
            qk_blocks = gla_cols // 6 // 1024
            col_blocks = ((qk_blocks, "plain", (gla_cols // 6 // GLA_HEADS) ** -0.5), (qk_blocks, "plain", 1.0),
                          (2 * qk_blocks, "plain", 1.0), (2 * qk_blocks, "silu", 1.0))
            proj, z = _proj_in(x, sc1, sh1, norm1_g[i], gla_w_in, j, col_blocks, wz=wz)
            o = _gla_attn(proj, z, w_up, gla_b_gate[j], GLA_HEADS)
            x = _out_proj(o, gla_w_out, j, x, g1)
        else:
            x = _pool_mixer(x, sc1, sh1, norm1_g[i], pool_w[j], pool_scale[j], g1)
        if i + 1 < depth:
            x, ffn_in, ffn_out = _ffn(x, sc2, sh2, norm2_g[i], g2, ffn_in, ffn_out, final_g, False,
                                      next_f32=(ffn_w_in, ffn_w_out, i + 1))
        else:
            x = _ffn(x, sc2, sh2, norm2_g[i], g2, ffn_in, ffn_out, final_g, True)
    return x
```

```python
import functools

import jax
import jax.numpy as jnp
from jax import lax
from jax.experimental import pallas as pl
from jax.experimental.pallas import tpu as pltpu

F32 = jnp.float32
BF16 = jnp.bfloat16

EPS = 1e-6
N_MIXERS = 4
RET_HEADS = 8
ROPE_BASE = 10000.0
CONV_WIDTH = 3
GLA_HEADS = 4
GLA_GATE_RANK = 16
GLA_GATE_TAU = 16.0
GLA_CHUNK = 64
POOL_WINDOWS = (2, 4, 8, 16)

V7X_VMEM_BYTES = 64 * 1024 * 1024
V7X_LANES = 128
V7X_SUBLANES = 8
BF16_ROWS = 2 * V7X_SUBLANES
MIB = 1024 * 1024

ROWS_ROPE = 1024
ROWS_PROJ = 1024
ROWS_OUT_PROJ = 512
ROWS_RET = 1024
RET_CHUNK = 256
ROWS_GLA = 256
ROWS_CONV = 1024
ROWS_POOL = 512
ROWS_FFN = 1024
PROLOGUE_ROWS = 256


def _vmem_limit(pipelined_block_bytes, scratch_bytes, temp_bytes):
    need = 2 * pipelined_block_bytes + scratch_bytes + temp_bytes + 2 * MIB
    return int(min(need, V7X_VMEM_BYTES - 6 * MIB))


def _params(semantics, vmem_bytes):
    return pltpu.CompilerParams(dimension_semantics=semantics, vmem_limit_bytes=vmem_bytes)


def _norm_mod(x, gain, scale, shift):
    ms = jnp.mean(x * x, axis=-1, keepdims=True)
    return (x * lax.rsqrt(ms + EPS)) * gain * (1.0 + scale) + shift


def _norm_mod_tile(x_ref, gs_ref, sh_ref, h_ref, rows):
    x = x_ref[rows, :]
    ms = jnp.mean(x * x, axis=-1, keepdims=True)
    h_ref[rows, :] = ((x * lax.rsqrt(ms + EPS)) * gs_ref[...] + sh_ref[...]).astype(h_ref.dtype)


def _silu(x):
    return x * jax.nn.sigmoid(x)


def _head_rms(o):
    return o * lax.rsqrt(jnp.mean(o * o, axis=-1, keepdims=True) + EPS)


def _mod_kernel(c_ref, w_ref, b_ref, o_ref):
    c_act = _silu(c_ref[...]).astype(BF16)
    o_ref[...] = jnp.dot(c_act, w_ref[...].astype(BF16), preferred_element_type=F32) + b_ref[...]


def _modulation(c, w_mod, b_mod):
    depth, d, n = w_mod.shape
    b = c.shape[0]
    tn = 1024
    return pl.pallas_call(
        _mod_kernel,
        grid=(depth, n // tn),
        in_specs=[
            pl.BlockSpec((b, d), lambda l, j: (0, 0)),
            pl.BlockSpec((None, d, tn), lambda l, j: (l, 0, j)),
            pl.BlockSpec((None, 1, tn), lambda l, j: (l, 0, j)),
        ],
        out_specs=pl.BlockSpec((None, b, tn), lambda l, j: (l, 0, j)),
        out_shape=jax.ShapeDtypeStruct((depth, b, n), F32),
        compiler_params=_params(("parallel", "parallel"), _vmem_limit(d * tn * 4, 0, d * tn * 2)),
        name="adaln_mod",
    )(c, w_mod, b_mod.reshape(depth, 1, n))


def _rope_kernel(pos_ref, invf_ref, cos_ref, sin_ref):
    ang = pos_ref[...].astype(F32) * invf_ref[...]
    cos_ref[...] = jnp.cos(ang)
    sin_ref[...] = jnp.sin(ang)


def _rope_tables(positions, half):
    b, s = positions.shape
    inv_freq = jnp.power(ROPE_BASE, -jnp.linspace(0.0, 1.0, half, dtype=F32)).reshape(1, half)
    ts = min(s, ROWS_ROPE)
    spec = pl.BlockSpec((None, ts, half), lambda i, j: (i, j, 0))
    return pl.pallas_call(
        _rope_kernel,
        grid=(b, s // ts),
        in_specs=[pl.BlockSpec((None, ts, 1), lambda i, j: (i, j, 0)), pl.BlockSpec((1, half), lambda i, j: (0, 0))],
        out_specs=[spec, spec],
        out_shape=[jax.ShapeDtypeStruct((b, s, half), F32)] * 2,
        compiler_params=_params(("parallel", "parallel"), _vmem_limit(3 * ts * V7X_LANES * 4, 0, 8 * MIB)),
        name="rope_tables",
    )(positions.reshape(b, s, 1), inv_freq)


def _rope_apply(t, cos, sin):
    half = cos.shape[-1]
    parts = []
    for c in range(0, t.shape[-1], 2 * half):
        t1, t2 = t[:, c:c + half], t[:, c + half:c + 2 * half]
        parts += [t1 * cos - t2 * sin, t2 * cos + t1 * sin]
    return jnp.concatenate(parts, axis=-1)


def _proj_in_kernel(*refs, col_blocks, with_rope, with_z):
    x_ref, sc_ref, sh_ref, g_ref, w_ref = refs[:5]
    rest = list(refs[5:])
    cos_ref, sin_ref = (rest.pop(0), rest.pop(0)) if with_rope else (None, None)
    wz_ref = rest.pop(0) if with_z else None
    o_ref = rest.pop(0)
    z_ref = rest.pop(0) if with_z else None
    h_ref, gs_ref = rest
    k = pl.program_id(2)

    def emit(kind, scale, rows=slice(None)):
        y = jnp.dot(h_ref[rows, :], w_ref[...], preferred_element_type=F32)
        if kind == "rope":
            y = _rope_apply(y, cos_ref[rows, :], sin_ref[rows, :])
        elif kind == "silu":
            y = _silu(y)
        if scale != 1.0:
            y = y * scale
        o_ref[rows, :] = y.astype(o_ref.dtype)

    @pl.when(k == 0)
    def _():
        gs_ref[...] = g_ref[...] * (1.0 + sc_ref[...])
        tm = x_ref.shape[0]
        group = min(PROLOGUE_ROWS, tm)
        for r in range(0, tm, group):
            for t in range(r, r + group, BF16_ROWS):
                _norm_mod_tile(x_ref, gs_ref, sh_ref, h_ref, slice(t, t + BF16_ROWS))
            emit(col_blocks[0][1], col_blocks[0][2], slice(r, r + group))
        if with_z:
            z_ref[...] = jnp.dot(h_ref[...], wz_ref[...], preferred_element_type=F32).astype(z_ref.dtype)

    start = 0
    for n_blocks, kind, scale in col_blocks:
        pl.when(jnp.logical_and(k >= max(start, 1), k < start + n_blocks))(functools.partial(emit, kind, scale))
        start += n_blocks


def _proj_in(x, scale, shift, gain, w_stack, layer, col_blocks, rope=None, wz=None):
    b, s, d = x.shape
    tm = min(s, ROWS_PROJ)
    tn = 1024
    n_cols = tn * sum(n for n, _, _ in col_blocks)
    with_z = wz is not None
    with_rope = rope is not None
    vec = pl.BlockSpec((None, 1, d), lambda i, j, k: (i, 0, 0))
    in_specs = [
        pl.BlockSpec((None, tm, d), lambda i, j, k: (i, j, 0)),
        vec,
        vec,
        pl.BlockSpec((1, d), lambda i, j, k: (0, 0)),
        pl.BlockSpec((None, d, tn), lambda i, j, k: (layer, 0, k)),
    ]
    out_specs = [pl.BlockSpec((None, tm, tn), lambda i, j, k: (i, j, k))]
    out_shape = [jax.ShapeDtypeStruct((b, s, n_cols), BF16)]
    args = [x, scale, shift, gain.reshape(1, d), w_stack]
    if with_rope:
        half = rope[0].shape[-1]
        in_specs += [pl.BlockSpec((None, tm, half), lambda i, j, k: (i, j, 0))] * 2
        args += list(rope)
    if with_z:
        in_specs.append(pl.BlockSpec((d, V7X_LANES), lambda i, j, k: (0, 0)))
        out_specs.append(pl.BlockSpec((None, tm, V7X_LANES), lambda i, j, k: (i, j, 0)))
        out_shape.append(jax.ShapeDtypeStruct((b, s, V7X_LANES), BF16))
        args.append(wz)
    outs = pl.pallas_call(
        functools.partial(_proj_in_kernel, col_blocks=col_blocks, with_rope=with_rope, with_z=with_z),
        grid=(b, s // tm, n_cols // tn),
        in_specs=in_specs,
        out_specs=out_specs,
        out_shape=out_shape,
        scratch_shapes=[pltpu.VMEM((tm, d), BF16), pltpu.VMEM((1, d), F32)],
        compiler_params=_params(
            ("parallel", "parallel", "arbitrary"),
            _vmem_limit(tm * d * 4 + d * tn * 2 + tm * tn * 2 + (d + 5 * tm) * V7X_LANES * 2, tm * d * 2,
                        2 * tm * tn * 4),
        ),
        name="proj_in",
    )(*args)
    return outs if with_z else outs[0]


def _out_proj_kernel(a_ref, w_ref, x_ref, gate_ref, o_ref):
    y = jnp.dot(a_ref[...], w_ref[...], preferred_element_type=F32)
    o_ref[...] = x_ref[...] + gate_ref[...] * y


def _out_proj(a, w_stack, layer, x, gate):
    b, s, k = a.shape
    d = w_stack.shape[2]
    tm = min(s, ROWS_OUT_PROJ)
    tile = pl.BlockSpec((None, tm, d), lambda i, j: (i, j, 0))
    return pl.pallas_call(
        _out_proj_kernel,
        grid=(b, s // tm),
        in_specs=[
            pl.BlockSpec((None, tm, k), lambda i, j: (i, j, 0)),
            pl.BlockSpec((None, k, d), lambda i, j: (layer, 0, 0), pipeline_mode=pl.Buffered(1)),
            tile,
            pl.BlockSpec((None, 1, d), lambda i, j: (i, 0, 0)),
        ],
        out_specs=tile,
        out_shape=jax.ShapeDtypeStruct((b, s, d), F32),
        compiler_params=_params(
            ("parallel", "parallel"),
            _vmem_limit(tm * k * 2 + 2 * tm * d * 4, k * d * 2, tm * d * 4),
        ),
        name="out_proj",
    )(a, w_stack, x, gate)


def _ret_attn_kernel(lg_ref, q_ref, k_ref, v_ref, g_ref, o_ref, state_ref, *, chunk):
    @pl.when(pl.program_id(2) == 0)
    def _():
        state_ref[...] = jnp.zeros_like(state_ref)

    tile = q_ref.shape[0]
    lg = lg_ref[:, 0:1]
    ii = lax.broadcasted_iota(jnp.int32, (chunk, chunk), 0)
    jj = lax.broadcasted_iota(jnp.int32, (chunk, chunk), 1)
    decay = jnp.where(ii >= jj, jnp.exp((ii - jj).astype(F32) * lg), 0.0)
    r = lax.broadcasted_iota(jnp.int32, (chunk, 1), 0).astype(F32)
    row_dec = jnp.exp((r + 1.0) * lg)
    col_dec = jnp.exp((chunk - 1.0 - r) * lg)
    chunk_dec = jnp.exp(chunk * lg)

    state = state_ref[...]
    for c in range(tile // chunk):
        sl = pl.ds(c * chunk, chunk)
        q = q_ref[sl, :]
        k = k_ref[sl, :]
        v = v_ref[sl, :]
        scores = lax.dot_general(q, k, (((1,), (1,)), ((), ())), preferred_element_type=F32)
        o = jnp.dot((scores * decay).astype(BF16), v, preferred_element_type=F32)
        o += jnp.dot((q.astype(F32) * row_dec).astype(BF16), state.astype(BF16), preferred_element_type=F32)
        k_carry = (k.astype(F32) * col_dec).astype(BF16)
        state = chunk_dec * state + lax.dot_general(k_carry, v, (((0,), (0,)), ((), ())),
                                                    preferred_element_type=F32)
        o_ref[sl, :] = (g_ref[sl, :].astype(F32) * _head_rms(o)).astype(o_ref.dtype)
    state_ref[...] = state


def _ret_attn(qkvg, heads):
    b, s, n = qkvg.shape
    dk = n // (6 * heads)
    dv = 2 * dk
    tile = min(s, ROWS_RET)
    chunk = min(tile, RET_CHUNK)
    log_gamma = jnp.log1p(-jnp.exp2(-5.0 - jnp.arange(heads, dtype=F32)))
    lg = jnp.broadcast_to(log_gamma[:, None, None], (heads, 1, V7X_LANES))
    return pl.pallas_call(
        functools.partial(_ret_attn_kernel, chunk=chunk),
        grid=(b, heads, s // tile),
        in_specs=[
            pl.BlockSpec((None, 1, V7X_LANES), lambda i, h, j: (h, 0, 0)),
            pl.BlockSpec((None, tile, dk), lambda i, h, j: (i, j, h)),
            pl.BlockSpec((None, tile, dk), lambda i, h, j: (i, j, heads + h)),
            pl.BlockSpec((None, tile, dv), lambda i, h, j: (i, j, heads + h)),
            pl.BlockSpec((None, tile, dv), lambda i, h, j: (i, j, 2 * heads + h)),
        ],
        out_specs=pl.BlockSpec((None, tile, dv), lambda i, h, j: (i, j, h)),
        out_shape=jax.ShapeDtypeStruct((b, s, heads * dv), BF16),
        scratch_shapes=[pltpu.VMEM((dk, dv), F32)],
        compiler_params=_params(
            ("parallel", "parallel", "arbitrary"),
            _vmem_limit(tile * (2 * dk + 3 * dv) * 2, dk * dv * 4, 16 * MIB),
        ),
        name="retention_core",
    )(lg, qkvg, qkvg, qkvg, qkvg)


def _gla_attn_kernel(q_ref, k_ref, v_ref, g_ref, z_ref, wg_ref, bg_ref, o_ref, state_ref, *, chunk):
    @pl.when(pl.program_id(1) == 0)
    def _():
        state_ref[...] = jnp.zeros_like(state_ref)

    heads, dv, dk = state_ref.shape
    tile = q_ref.shape[0]
    shift = chunk.bit_length() - 1
    pre = jnp.dot(z_ref[...], wg_ref[...], preferred_element_type=F32) + bg_ref[...]
    log_a = (jnp.minimum(pre, 0.0) - jnp.log1p(jnp.exp(-jnp.abs(pre)))) / GLA_GATE_TAU
    ii = lax.broadcasted_iota(jnp.int32, (tile, tile), 0)
    jj = lax.broadcasted_iota(jnp.int32, (tile, tile), 1)
    same_chunk = (ii >> shift) == (jj >> shift)
    causal = jnp.logical_and(same_chunk, ii >= jj)
    sums = jnp.concatenate([jnp.where(causal, 1.0, 0.0), jnp.where(same_chunk, 1.0, 0.0)], axis=0).astype(BF16)
    hi = log_a.astype(BF16)
    rest = log_a - hi.astype(F32)
    mid = rest.astype(BF16)
    lo = (rest - mid.astype(F32)).astype(BF16)
    bb = (jnp.dot(sums, hi, preferred_element_type=F32) + jnp.dot(sums, mid, preferred_element_type=F32)
          + jnp.dot(sums, lo, preferred_element_type=F32))
    b = bb[:tile]
    b_last = bb[tile:]

    k = k_ref[...].astype(F32)
    q_dec = (q_ref[...].astype(F32) * jnp.exp(b)).astype(BF16)
    k_dec = (k * jnp.exp(-b)).astype(BF16)
    k_carry = (k * jnp.exp(b_last - b)).astype(BF16)
    chunk_dec = [jnp.exp(b_last[c * chunk:c * chunk + 1, :]) for c in range(tile // chunk)]

    for h in range(heads):
        kc = slice(h * dk, (h + 1) * dk)
        vc = slice(h * dv, (h + 1) * dv)
        v = v_ref[:, vc]
        scores = lax.dot_general(q_dec[:, kc], k_dec[:, kc], (((1,), (1,)), ((), ())), preferred_element_type=F32)
        scores = jnp.where(causal, scores, 0.0).astype(BF16)
        o = jnp.dot(scores, v, preferred_element_type=F32)
        state = state_ref[h]
        inter = []
        for c in range(tile // chunk):
            rows = slice(c * chunk, (c + 1) * chunk)
            inter.append(lax.dot_general(q_dec[rows, kc], state.astype(BF16), (((1,), (1,)), ((), ())),
                                         preferred_element_type=F32))
            update = lax.dot_general(v[rows], k_carry[rows, kc], (((0,), (0,)), ((), ())),
                                     preferred_element_type=F32)
            state = chunk_dec[c][:, kc] * state + update
        state_ref[h] = state
        o = o + jnp.concatenate(inter, axis=0)
        o_ref[:, vc] = (g_ref[:, vc].astype(F32) * _head_rms(o)).astype(o_ref.dtype)


def _gla_attn(proj, z, w_gate_up, b_gate, heads):
    b, s, n = proj.shape
    dk = n // (6 * heads)
    dv = 2 * dk
    tile = min(s, ROWS_GLA)
    return pl.pallas_call(
        functools.partial(_gla_attn_kernel, chunk=GLA_CHUNK),
        grid=(b, s // tile),
        in_specs=[
            pl.BlockSpec((None, tile, heads * dk), lambda i, j: (i, j, 0)),
            pl.BlockSpec((None, tile, heads * dk), lambda i, j: (i, j, 1)),
            pl.BlockSpec((None, tile, heads * dv), lambda i, j: (i, j, 1)),
            pl.BlockSpec((None, tile, heads * dv), lambda i, j: (i, j, 2)),
            pl.BlockSpec((None, tile, V7X_LANES), lambda i, j: (i, j, 0)),
            pl.BlockSpec((V7X_LANES, heads * dk), lambda i, j: (0, 0)),
            pl.BlockSpec((1, heads * dk), lambda i, j: (0, 0)),
        ],
        out_specs=pl.BlockSpec((None, tile, heads * dv), lambda i, j: (i, j, 0)),
        out_shape=jax.ShapeDtypeStruct((b, s, heads * dv), BF16),
        scratch_shapes=[pltpu.VMEM((heads, dv, dk), F32)],
        compiler_params=_params(
            ("parallel", "arbitrary"),
            _vmem_limit(tile * heads * (2 * dk + 3 * dv) * 2 + 2 * V7X_LANES * heads * dk * 2,
                        heads * dk * dv * 4, 24 * MIB),
        ),
        name="gla_core",
    )(proj, proj, proj, proj, z, w_gate_up, b_gate.reshape(1, -1))


def _conv_in_kernel(x_ref, sc_ref, sh_ref, g_ref, wb_ref, wc_ref, wu_ref, cw_ref, o_ref, h_ref, gs_ref, carry_ref):
    j = pl.program_id(1)
    n = pl.program_id(2)
    tm, tn = o_ref.shape

    prev_tile = jnp.where(j > 0, carry_ref[n], 0.0)

    def conv_rows(r0, nrows, prev_rows):
        rows = slice(r0, r0 + nrows)
        h = h_ref[rows, :]
        row = lax.broadcasted_iota(jnp.int32, (nrows, 1), 0)
        last = {}
        for c in range(0, tn, 256):
            cols = slice(c, c + 256)
            b_gate = jnp.dot(h, wb_ref[:, cols], preferred_element_type=F32)
            c_gate = jnp.dot(h, wc_ref[:, cols], preferred_element_type=F32)
            u = c_gate * jnp.dot(h, wu_ref[:, cols], preferred_element_type=F32)
            prev = prev_rows[c]
            last[c] = u[nrows - V7X_SUBLANES:, :]
            u1 = jnp.where(row == 0, prev[7:8, :], pltpu.roll(u, 1, axis=0))
            u2 = jnp.where(row == 0, prev[6:7, :],
                           jnp.where(row == 1, prev[7:8, :], pltpu.roll(u, 2, axis=0)))
            y = cw_ref[0:1, cols] * u2 + cw_ref[1:2, cols] * u1 + cw_ref[2:3, cols] * u
            o_ref[rows, cols] = (b_gate * y).astype(o_ref.dtype)
        return last

    def store_carry(last):
        for c, rows8 in last.items():
            carry_ref[n, :, c:c + 256] = rows8

    tile_prev = {c: prev_tile[:, c:c + 256] for c in range(0, tn, 256)}

    @pl.when(n == 0)
    def _():
        gs_ref[...] = g_ref[...] * (1.0 + sc_ref[...])
        prev_rows = tile_prev
        group = min(PROLOGUE_ROWS, tm)
        for r in range(0, tm, group):
            for t in range(r, r + group, BF16_ROWS):
                _norm_mod_tile(x_ref, gs_ref, sh_ref, h_ref, slice(t, t + BF16_ROWS))
            prev_rows = conv_rows(r, group, prev_rows)
        store_carry(prev_rows)

    @pl.when(n > 0)
    def _():
        store_carry(conv_rows(0, tm, tile_prev))


def _conv_in(x, scale, shift, gain, w_stack, conv_w_stack, layer):
    b, s, d = x.shape
    tm = min(s, ROWS_CONV)
    tn = 512
    nb = d // tn
    vec = pl.BlockSpec((None, 1, d), lambda i, j, n: (i, 0, 0))

    def w_spec(part):
        return pl.BlockSpec((None, d, tn), lambda i, j, n: (layer, 0, part * nb + n))

    return pl.pallas_call(
        _conv_in_kernel,
        grid=(b, s // tm, nb),
        in_specs=[
            pl.BlockSpec((None, tm, d), lambda i, j, n: (i, j, 0)),
            vec,
            vec,
            pl.BlockSpec((1, d), lambda i, j, n: (0, 0)),
            w_spec(0),
            w_spec(1),
            w_spec(2),
            pl.BlockSpec((None, CONV_WIDTH, tn), lambda i, j, n: (layer, 0, n)),
        ],
        out_specs=pl.BlockSpec((None, tm, tn), lambda i, j, n: (i, j, n)),
        out_shape=jax.ShapeDtypeStruct((b, s, d), BF16),
        scratch_shapes=[pltpu.VMEM((tm, d), BF16), pltpu.VMEM((1, d), F32),
                        pltpu.VMEM((nb, V7X_SUBLANES, tn), F32)],
        compiler_params=_params(
            ("parallel", "arbitrary", "arbitrary"),
            _vmem_limit(tm * d * 4 + 3 * d * tn * 2 + tm * tn * 2, tm * d * 2, 8 * tm * tn * 4),
        ),
        name="conv_in",
    )(x, scale, shift, gain.reshape(1, d), w_stack, w_stack, w_stack, conv_w_stack)


def _pool_kernel(x_ref, sc_ref, sh_ref, g_ref, w_ref, ps_ref, gate_ref, o_ref, carry_ref):
    j = pl.program_id(1)
    tm, d = x_ref.shape
    halo = carry_ref.shape[0]
    group = d // len(POOL_WINDOWS)
    x = x_ref[...]
    h = _norm_mod(x, g_ref[...], sc_ref[...], sh_ref[...])
    prev = jnp.where(j > 0, carry_ref[...], 0.0)
    carry_ref[...] = h[tm - halo:, :]
    t = j * tm + lax.broadcasted_iota(jnp.int32, (tm, 1), 0)

    for gi, win in enumerate(POOL_WINDOWS):
        cols = slice(gi * group, (gi + 1) * group)
        hg = h[:, cols]
        ext = jnp.concatenate([prev[:, cols], hg], axis=0)
        k = 1
        while k < win:
            ext = ext + pltpu.roll(ext, k, axis=0)
            k *= 2
        count = jnp.minimum(t + 1, win).astype(F32)
        mixed = ext[halo:, :] / count - hg
        y = jnp.dot(mixed.astype(BF16), w_ref[gi], preferred_element_type=F32)
        o_ref[:, cols] = x[:, cols] + gate_ref[:, cols] * (y * ps_ref[:, cols])


def _pool_mixer(x, scale, shift, gain, pool_w, pool_scale, gate):
    b, s, d = x.shape
    groups, p, _ = pool_w.shape
    tm = min(s, ROWS_POOL)
    halo = 2 * V7X_SUBLANES
    vec = pl.BlockSpec((None, 1, d), lambda i, j: (i, 0, 0))
    row = pl.BlockSpec((1, d), lambda i, j: (0, 0))
    tile = pl.BlockSpec((None, tm, d), lambda i, j: (i, j, 0))
    return pl.pallas_call(
        _pool_kernel,
        grid=(b, s // tm),
        in_specs=[tile, vec, vec, row, pl.BlockSpec((groups, p, p), lambda i, j: (0, 0, 0)), row, vec],
        out_specs=tile,
        out_shape=jax.ShapeDtypeStruct((b, s, d), F32),
        scratch_shapes=[pltpu.VMEM((halo, d), F32)],
        compiler_params=_params(
            ("parallel", "arbitrary"),
            _vmem_limit(2 * tm * d * 4 + groups * p * p * 2, halo * d * 4, 6 * tm * d * 4),
        ),
        name="pool_mixer",
    )(x, scale, shift, gain.reshape(1, d), pool_w, pool_scale.reshape(1, d), gate)


def _ffn_kernel(x_ref, sc_ref, sh_ref, g_ref, gate_ref, wa_ref, wb_ref, wo_ref, fg_ref, *rest, final_norm, cast_next):
    if cast_next:
        next_in_ref, next_out_ref, o_ref, next_in_bf_ref, next_out_bf_ref, h_ref, gs_ref = rest
    else:
        o_ref, h_ref, gs_ref = rest
    f = pl.program_id(2)

    def ffn_part(rows):
        h = h_ref[rows, :]
        acts = []
        for c in range(0, wa_ref.shape[1], 256):
            a = jnp.dot(h, wa_ref[:, c:c + 256], preferred_element_type=F32)
            b = jnp.dot(h, wb_ref[:, c:c + 256], preferred_element_type=F32)
            acts.append((_silu(a) * b).astype(BF16))
        act = jnp.concatenate(acts, axis=-1)
        return gate_ref[...] * jnp.dot(act, wo_ref[...], preferred_element_type=F32)

    def cast_next_slab():
        if cast_next:
            next_in_bf_ref[...] = next_in_ref[...].astype(BF16)
            next_out_bf_ref[...] = next_out_ref[...].astype(BF16)

    @pl.when(f == 0)
    def _():
        gs_ref[...] = g_ref[...] * (1.0 + sc_ref[...])
        group = min(PROLOGUE_ROWS, x_ref.shape[0])
        for r in range(0, x_ref.shape[0], group):
            rows = slice(r, r + group)
            for t in range(r, r + group, BF16_ROWS):
                _norm_mod_tile(x_ref, gs_ref, sh_ref, h_ref, slice(t, t + BF16_ROWS))
            o_ref[rows, :] = x_ref[rows, :] + ffn_part(rows)
        cast_next_slab()

    @pl.when(f > 0)
    def _():
        o_ref[...] += ffn_part(slice(None))
        cast_next_slab()

    if final_norm:
        @pl.when(f == pl.num_programs(2) - 1)
        def _():
            for r in range(0, o_ref.shape[0], 4 * V7X_SUBLANES):
                rows = slice(r, r + 4 * V7X_SUBLANES)
                o_ref[rows, :] = _head_rms(o_ref[rows, :]) * fg_ref[...]


def _ffn(x, scale, shift, gain, gate, w_in, w_out, final_gain, final_norm, next_f32=None):
    b, s, d = x.shape
    dff = w_out.shape[0]
    tm = min(s, ROWS_FFN)
    tf = 512
    nf = dff // tf
    nj = s // tm
    vec = pl.BlockSpec((None, 1, d), lambda i, j, f: (i, 0, 0))
    row = pl.BlockSpec((1, d), lambda i, j, f: (0, 0))
    tile = pl.BlockSpec((None, tm, d), lambda i, j, f: (i, j, 0))
    in_specs = [
        tile,
        vec,
        vec,
        row,
        vec,
        pl.BlockSpec((d, tf), lambda i, j, f: (0, f)),
        pl.BlockSpec((d, tf), lambda i, j, f: (0, nf + f)),
        pl.BlockSpec((tf, d), lambda i, j, f: (f, 0)),
        row,
    ]
    out_specs = [tile]
    out_shape = [jax.ShapeDtypeStruct((b, s, d), F32)]
    args = [x, scale, shift, gain.reshape(1, d), gate, w_in, w_in, w_out, final_gain.reshape(1, d)]
    slab_bytes = 0
    if next_f32 is not None:
        nw_in, nw_out, nl = next_f32
        slabs = 2 * dff // V7X_LANES
        out_rows = dff // slabs
        assert b * nj * nf >= slabs and out_rows % BF16_ROWS == 0

        def slab(i, j, f):
            return jnp.minimum((i * nj + j) * nf + f, slabs - 1)

        in_specs += [pl.BlockSpec((None, d, V7X_LANES), lambda i, j, f: (nl, 0, slab(i, j, f))),
                     pl.BlockSpec((None, out_rows, d), lambda i, j, f: (nl, slab(i, j, f), 0))]
        out_specs += [pl.BlockSpec((d, V7X_LANES), lambda i, j, f: (0, slab(i, j, f))),
                      pl.BlockSpec((out_rows, d), lambda i, j, f: (slab(i, j, f), 0))]
        out_shape += [jax.ShapeDtypeStruct((d, 2 * dff), BF16), jax.ShapeDtypeStruct((dff, d), BF16)]
        args += [nw_in, nw_out]
        slab_bytes = (d * V7X_LANES + out_rows * d) * 6
    outs = pl.pallas_call(
        functools.partial(_ffn_kernel, final_norm=final_norm, cast_next=next_f32 is not None),
        grid=(b, nj, nf),
        in_specs=in_specs,
        out_specs=out_specs,
        out_shape=out_shape,
        scratch_shapes=[pltpu.VMEM((tm, d), BF16), pltpu.VMEM((1, d), F32)],
        compiler_params=_params(
            ("arbitrary", "arbitrary", "arbitrary"),
            _vmem_limit(2 * tm * d * 4 + 3 * d * tf * 2 + slab_bytes, tm * d * 2, 4 * tm * 256 * 4),
        ),
        name="swiglu_ffn",
    )(*args)
    return outs if next_f32 is not None else outs[0]


def kernel(x, c, positions, w_mod, b_mod, norm1_g, norm2_g, ret_w_in, ret_w_out, conv_w_in, conv_w, conv_w_out,
           gla_w_in, gla_w_gate_up, gla_b_gate, gla_w_out, pool_w, pool_scale, ffn_w_in, ffn_w_out, final_g):
    b, s, d = x.shape
    depth = w_mod.shape[0]
    assert depth >= 1 and d % (len(POOL_WINDOWS) * V7X_LANES) == 0 and s % GLA_CHUNK == 0

    mod = _modulation(c, w_mod, b_mod).reshape(depth, b, 6, 1, d)
    ret_dk = d // RET_HEADS
    cos, sin = _rope_tables(positions, ret_dk // 2)

    ret_w_in, ret_w_out, conv_w_in, conv_w_out, gla_w_in, gla_w_out, pool_w = (
        w.astype(BF16) for w in (ret_w_in, ret_w_out, conv_w_in, conv_w_out, gla_w_in, gla_w_out, pool_w))
    ffn_in, ffn_out = ffn_w_in[0].astype(BF16), ffn_w_out[0].astype(BF16)
    gla_cols = gla_w_in.shape[2] - GLA_GATE_RANK
    lane_pad = V7X_LANES - GLA_GATE_RANK

    for i in range(depth):
        sh1, sc1, g1, sh2, sc2, g2 = (mod[i, :, p] for p in range(6))
        m, j = i % N_MIXERS, i // N_MIXERS
        if m == 0:
            qk_blocks = ret_w_in.shape[2] // 6 // 1024
            col_blocks = ((qk_blocks, "rope", 1.0), (qk_blocks, "rope", ret_dk ** -0.5),
                          (2 * qk_blocks, "plain", 1.0), (2 * qk_blocks, "silu", 1.0))
            qkvg = _proj_in(x, sc1, sh1, norm1_g[i], ret_w_in, j, col_blocks, rope=(cos, sin))
            o = _ret_attn(qkvg, RET_HEADS)
            x = _out_proj(o, ret_w_out, j, x, g1)
        elif m == 1:
            z = _conv_in(x, sc1, sh1, norm1_g[i], conv_w_in, conv_w, j)
            x = _out_proj(z, conv_w_out, j, x, g1)
        elif m == 2:
            wz = jnp.pad(gla_w_in[j, :, gla_cols:], ((0, 0), (0, lane_pad)))
            w_up = jnp.pad(gla_w_gate_up[j], ((0, lane_pad), (0, 0))).astype(BF16)
            qk_blocks = gla_cols // 6 // 1024
            col_blocks = ((qk_blocks, "plain", (gla_cols // 6 // GLA_HEADS) ** -0.5), (qk_blocks, "plain", 1.0),
                          (2 * qk_blocks, "plain", 1.0), (2 * qk_blocks, "silu", 1.0))
            proj, z = _proj_in(x, sc1, sh1, norm1_g[i], gla_w_in, j, col_blocks, wz=wz)
            o = _gla_attn(proj, z, w_up, gla_b_gate[j], GLA_HEADS)
            x = _out_proj(o, gla_w_out, j, x, g1)
        else:
            x = _pool_mixer(x, sc1, sh1, norm1_g[i], pool_w[j], pool_scale[j], g1)
        if i + 1 < depth:
            x, ffn_in, ffn_out = _ffn(x, sc2, sh2, norm2_g[i], g2, ffn_in, ffn_out, final_g, False,
                                      next_f32=(ffn_w_in, ffn_w_out, i + 1))
        else:
            x = _ffn(x, sc2, sh2, norm2_g[i], g2, ffn_in, ffn_out, final_g, True)
    return x
```

```python
import functools

import jax
import jax.numpy as jnp
from jax import lax
from jax.experimental import pallas as pl
from jax.experimental.pallas import tpu as pltpu

F32 = jnp.float32
BF16 = jnp.bfloat16

EPS = 1e-6
N_MIXERS = 4
RET_HEADS = 8
ROPE_BASE = 10000.0
CONV_WIDTH = 3
GLA_HEADS = 4
GLA_GATE_RANK = 16
GLA_GATE_TAU = 16.0
GLA_CHUNK = 64
POOL_WINDOWS = (2, 4, 8, 16)

V7X_VMEM_BYTES = 64 * 1024 * 1024
V7X_LANES = 128
V7X_SUBLANES = 8
BF16_ROWS = 2 * V7X_SUBLANES
MIB = 1024 * 1024

ROWS_ROPE = 1024
ROWS_PROJ = 1024
ROWS_OUT_PROJ = 512
ROWS_RET = 1024
RET_CHUNK = 256
ROWS_GLA = 256
ROWS_CONV = 1024
ROWS_POOL = 512
ROWS_FFN = 1024
PROLOGUE_ROWS = 256


def _vmem_limit(pipelined_block_bytes, scratch_bytes, temp_bytes):
    need = 2 * pipelined_block_bytes + scratch_bytes + temp_bytes + 2 * MIB
    return int(min(need, V7X_VMEM_BYTES - 6 * MIB))


def _params(semantics, vmem_bytes):
    return pltpu.CompilerParams(dimension_semantics=semantics, vmem_limit_bytes=vmem_bytes)


def _norm_mod(x, gain, scale, shift):
    ms = jnp.mean(x * x, axis=-1, keepdims=True)
    return (x * lax.rsqrt(ms + EPS)) * gain * (1.0 + scale) + shift


def _norm_mod_tile(x_ref, gs_ref, sh_ref, h_ref, rows):
    x = x_ref[rows, :]
    ms = jnp.mean(x * x, axis=-1, keepdims=True)
    h_ref[rows, :] = ((x * lax.rsqrt(ms + EPS)) * gs_ref[...] + sh_ref[...]).astype(h_ref.dtype)


def _cast_plan(rows, cols, steps):
    options = []
    for size in range(BF16_ROWS, rows + 1, BF16_ROWS):
        if rows % size == 0 and rows // size <= steps:
            options.append((size * cols, 0, size, rows // size))
            break
    for size in range(V7X_LANES, cols + 1, V7X_LANES):
        if cols % size == 0 and cols // size <= steps:
            options.append((rows * size, 1, size, cols // size))
            break
    assert options, "weight cannot be cast in slabs within this grid"
    _, axis, size, count = min(options)
    return axis, size, count


def _side_casts(side_casts, steps, step_of):
    in_specs, out_specs, out_shapes, args, vmem = [], [], [], [], 0
    for stack, layer in side_casts:
        _, rows, cols = stack.shape
        axis, size, count = _cast_plan(rows, cols, steps)

        def slab(*ids, count=count):
            return jnp.minimum(step_of(*ids), count - 1)

        if axis == 0:
            in_specs.append(pl.BlockSpec((None, size, cols), lambda *ids, la=layer, sl=slab: (la, sl(*ids), 0)))
            out_specs.append(pl.BlockSpec((size, cols), lambda *ids, sl=slab: (sl(*ids), 0)))
            vmem += 2 * size * cols * 6
        else:
            in_specs.append(pl.BlockSpec((None, rows, size), lambda *ids, la=layer, sl=slab: (la, 0, sl(*ids))))
            out_specs.append(pl.BlockSpec((rows, size), lambda *ids, sl=slab: (0, sl(*ids))))
            vmem += 2 * rows * size * 6
        out_shapes.append(jax.ShapeDtypeStruct((rows, cols), BF16))
        args.append(stack)
    return in_specs, out_specs, out_shapes, args, vmem


def _cast_slabs(src_refs, dst_refs):
    for src, dst in zip(src_refs, dst_refs):
        dst[...] = src[...].astype(BF16)


def _silu(x):
    return x * jax.nn.sigmoid(x)


def _head_rms(o):
    return o * lax.rsqrt(jnp.mean(o * o, axis=-1, keepdims=True) + EPS)


def _mod_kernel(c_ref, w_ref, b_ref, o_ref):
    c_act = _silu(c_ref[...]).astype(BF16)
    o_ref[...] = jnp.dot(c_act, w_ref[...].astype(BF16), preferred_element_type=F32) + b_ref[...]


def _modulation(c, w_mod, b_mod):
    depth, d, n = w_mod.shape
    b = c.shape[0]
    tn = 1024
    return pl.pallas_call(
        _mod_kernel,
        grid=(depth, n // tn),
        in_specs=[
            pl.BlockSpec((b, d), lambda l, j: (0, 0)),
            pl.BlockSpec((None, d, tn), lambda l, j: (l, 0, j)),
            pl.BlockSpec((None, 1, tn), lambda l, j: (l, 0, j)),
        ],
        out_specs=pl.BlockSpec((None, b, tn), lambda l, j: (l, 0, j)),
        out_shape=jax.ShapeDtypeStruct((depth, b, n), F32),
        compiler_params=_params(("parallel", "parallel"), _vmem_limit(d * tn * 4, 0, d * tn * 2)),
        name="adaln_mod",
    )(c, w_mod, b_mod.reshape(depth, 1, n))


def _rope_kernel(pos_ref, invf_ref, cos_ref, sin_ref):
    ang = pos_ref[...].astype(F32) * invf_ref[...]
    cos_ref[...] = jnp.cos(ang)
    sin_ref[...] = jnp.sin(ang)


def _rope_tables(positions, half):
    b, s = positions.shape
    inv_freq = jnp.power(ROPE_BASE, -jnp.linspace(0.0, 1.0, half, dtype=F32)).reshape(1, half)
    ts = min(s, ROWS_ROPE)
    spec = pl.BlockSpec((None, ts, half), lambda i, j: (i, j, 0))
    return pl.pallas_call(
        _rope_kernel,
        grid=(b, s // ts),
        in_specs=[pl.BlockSpec((None, ts, 1), lambda i, j: (i, j, 0)), pl.BlockSpec((1, half), lambda i, j: (0, 0))],
        out_specs=[spec, spec],
        out_shape=[jax.ShapeDtypeStruct((b, s, half), F32)] * 2,
        compiler_params=_params(("parallel", "parallel"), _vmem_limit(3 * ts * V7X_LANES * 4, 0, 8 * MIB)),
        name="rope_tables",
    )(positions.reshape(b, s, 1), inv_freq)


def _rope_apply(t, cos, sin):
    half = cos.shape[-1]
    parts = []
    for c in range(0, t.shape[-1], 2 * half):
        t1, t2 = t[:, c:c + half], t[:, c + half:c + 2 * half]
        parts += [t1 * cos - t2 * sin, t2 * cos + t1 * sin]
    return jnp.concatenate(parts, axis=-1)


def _proj_in_kernel(*refs, col_blocks, with_rope, with_z, n_casts):
    x_ref, sc_ref, sh_ref, g_ref, w_ref = refs[:5]
    rest = list(refs[5:])
    cos_ref, sin_ref = (rest.pop(0), rest.pop(0)) if with_rope else (None, None)
    wz_ref = rest.pop(0) if with_z else None
    cast_srcs = [rest.pop(0) for _ in range(n_casts)]
    o_ref = rest.pop(0)
    z_ref = rest.pop(0) if with_z else None
    cast_dsts = [rest.pop(0) for _ in range(n_casts)]
    h_ref, gs_ref = rest
    k = pl.program_id(2)

    def emit(kind, scale, rows=slice(None)):
        y = jnp.dot(h_ref[rows, :], w_ref[...], preferred_element_type=F32)
        if kind == "rope":
            y = _rope_apply(y, cos_ref[rows, :], sin_ref[rows, :])
        elif kind == "silu":
            y = _silu(y)
        if scale != 1.0:
            y = y * scale
        o_ref[rows, :] = y.astype(o_ref.dtype)

    def emit_step(kind, scale):
        emit(kind, scale)
        _cast_slabs(cast_srcs, cast_dsts)

    @pl.when(k == 0)
    def _():
        gs_ref[...] = g_ref[...] * (1.0 + sc_ref[...])
        tm = x_ref.shape[0]
        group = min(PROLOGUE_ROWS, tm)
        for r in range(0, tm, group):
            for t in range(r, r + group, BF16_ROWS):
                _norm_mod_tile(x_ref, gs_ref, sh_ref, h_ref, slice(t, t + BF16_ROWS))
            emit(col_blocks[0][1], col_blocks[0][2], slice(r, r + group))
        if with_z:
            z_ref[...] = jnp.dot(h_ref[...], wz_ref[...], preferred_element_type=F32).astype(z_ref.dtype)
        _cast_slabs(cast_srcs, cast_dsts)

    start = 0
    for n_blocks, kind, scale in col_blocks:
        pl.when(jnp.logical_and(k >= max(start, 1), k < start + n_blocks))(functools.partial(emit_step, kind, scale))
        start += n_blocks


def _proj_in(x, scale, shift, gain, w, col_blocks, rope=None, wz=None, side_casts=()):
    b, s, d = x.shape
    tm = min(s, ROWS_PROJ)
    tn = 1024
    nj = s // tm
    nk = sum(n for n, _, _ in col_blocks)
    n_cols = tn * nk
    with_z = wz is not None
    with_rope = rope is not None
    vec = pl.BlockSpec((None, 1, d), lambda i, j, k: (i, 0, 0))
    in_specs = [
        pl.BlockSpec((None, tm, d), lambda i, j, k: (i, j, 0)),
        vec,
        vec,
        pl.BlockSpec((1, d), lambda i, j, k: (0, 0)),
        pl.BlockSpec((d, tn), lambda i, j, k: (0, k)),
    ]
    out_specs = [pl.BlockSpec((None, tm, tn), lambda i, j, k: (i, j, k))]
    out_shape = [jax.ShapeDtypeStruct((b, s, n_cols), BF16)]
    args = [x, scale, shift, gain.reshape(1, d), w]
    if with_rope:
        half = rope[0].shape[-1]
        in_specs += [pl.BlockSpec((None, tm, half), lambda i, j, k: (i, j, 0))] * 2
        args += list(rope)
    if with_z:
        in_specs.append(pl.BlockSpec((d, V7X_LANES), lambda i, j, k: (0, 0)))
        out_specs.append(pl.BlockSpec((None, tm, V7X_LANES), lambda i, j, k: (i, j, 0)))
        out_shape.append(jax.ShapeDtypeStruct((b, s, V7X_LANES), BF16))
        args.append(wz)
    c_in, c_out, c_shape, c_args, c_vmem = _side_casts(side_casts, b * nj * nk, lambda i, j, k: (i * nj + j) * nk + k)
    outs = pl.pallas_call(
        functools.partial(_proj_in_kernel, col_blocks=col_blocks, with_rope=with_rope, with_z=with_z,
                          n_casts=len(side_casts)),
        grid=(b, nj, nk),
        in_specs=in_specs + c_in,
        out_specs=out_specs + c_out,
        out_shape=out_shape + c_shape,
        scratch_shapes=[pltpu.VMEM((tm, d), BF16), pltpu.VMEM((1, d), F32)],
        compiler_params=_params(
            ("arbitrary", "arbitrary", "arbitrary"),
            _vmem_limit(tm * d * 4 + d * tn * 2 + tm * tn * 2 + (d + 5 * tm) * V7X_LANES * 2, tm * d * 2 + c_vmem,
                        2 * tm * tn * 4),
        ),
        name="proj_in",
    )(*args, *c_args)
    return list(outs)


def _out_proj_kernel(a_ref, w_ref, x_ref, gate_ref, o_ref):
    y = jnp.dot(a_ref[...], w_ref[...], preferred_element_type=F32)
    o_ref[...] = x_ref[...] + gate_ref[...] * y


def _out_proj(a, w, x, gate):
    b, s, k = a.shape
    d = w.shape[1]
    tm = min(s, ROWS_OUT_PROJ)
    tile = pl.BlockSpec((None, tm, d), lambda i, j: (i, j, 0))
    return pl.pallas_call(
        _out_proj_kernel,
        grid=(b, s // tm),
        in_specs=[
            pl.BlockSpec((None, tm, k), lambda i, j: (i, j, 0)),
            pl.BlockSpec((k, d), lambda i, j: (0, 0), pipeline_mode=pl.Buffered(1)),
            tile,
            pl.BlockSpec((None, 1, d), lambda i, j: (i, 0, 0)),
        ],
        out_specs=tile,
        out_shape=jax.ShapeDtypeStruct((b, s, d), F32),
        compiler_params=_params(
            ("parallel", "parallel"),
            _vmem_limit(tm * k * 2 + 2 * tm * d * 4, k * d * 2, tm * d * 4),
        ),
        name="out_proj",
    )(a, w, x, gate)


def _ret_attn_kernel(lg_ref, q_ref, k_ref, v_ref, g_ref, o_ref, state_ref, *, chunk):
    @pl.when(pl.program_id(2) == 0)
    def _():
        state_ref[...] = jnp.zeros_like(state_ref)

    tile = q_ref.shape[0]
    lg = lg_ref[:, 0:1]
    ii = lax.broadcasted_iota(jnp.int32, (chunk, chunk), 0)
    jj = lax.broadcasted_iota(jnp.int32, (chunk, chunk), 1)
    decay = jnp.where(ii >= jj, jnp.exp((ii - jj).astype(F32) * lg), 0.0)
    r = lax.broadcasted_iota(jnp.int32, (chunk, 1), 0).astype(F32)
    row_dec = jnp.exp((r + 1.0) * lg)
    col_dec = jnp.exp((chunk - 1.0 - r) * lg)
    chunk_dec = jnp.exp(chunk * lg)

    state = state_ref[...]
    for c in range(tile // chunk):
        sl = pl.ds(c * chunk, chunk)
        q = q_ref[sl, :]
        k = k_ref[sl, :]
        v = v_ref[sl, :]
        scores = lax.dot_general(q, k, (((1,), (1,)), ((), ())), preferred_element_type=F32)
        o = jnp.dot((scores * decay).astype(BF16), v, preferred_element_type=F32)
        o += jnp.dot((q.astype(F32) * row_dec).astype(BF16), state.astype(BF16), preferred_element_type=F32)
        k_carry = (k.astype(F32) * col_dec).astype(BF16)
        state = chunk_dec * state + lax.dot_general(k_carry, v, (((0,), (0,)), ((), ())),
                                                    preferred_element_type=F32)
        o_ref[sl, :] = (g_ref[sl, :].astype(F32) * _head_rms(o)).astype(o_ref.dtype)
    state_ref[...] = state


def _ret_attn(qkvg, heads):
    b, s, n = qkvg.shape
    dk = n // (6 * heads)
    dv = 2 * dk
    tile = min(s, ROWS_RET)
    chunk = min(tile, RET_CHUNK)
    log_gamma = jnp.log1p(-jnp.exp2(-5.0 - jnp.arange(heads, dtype=F32)))
    lg = jnp.broadcast_to(log_gamma[:, None, None], (heads, 1, V7X_LANES))
    return pl.pallas_call(
        functools.partial(_ret_attn_kernel, chunk=chunk),
        grid=(b, heads, s // tile),
        in_specs=[
            pl.BlockSpec((None, 1, V7X_LANES), lambda i, h, j: (h, 0, 0)),
            pl.BlockSpec((None, tile, dk), lambda i, h, j: (i, j, h)),
            pl.BlockSpec((None, tile, dk), lambda i, h, j: (i, j, heads + h)),
            pl.BlockSpec((None, tile, dv), lambda i, h, j: (i, j, heads + h)),
            pl.BlockSpec((None, tile, dv), lambda i, h, j: (i, j, 2 * heads + h)),
        ],
        out_specs=pl.BlockSpec((None, tile, dv), lambda i, h, j: (i, j, h)),
        out_shape=jax.ShapeDtypeStruct((b, s, heads * dv), BF16),
        scratch_shapes=[pltpu.VMEM((dk, dv), F32)],
        compiler_params=_params(
            ("parallel", "parallel", "arbitrary"),
            _vmem_limit(tile * (2 * dk + 3 * dv) * 2, dk * dv * 4, 16 * MIB),
        ),
        name="retention_core",
    )(lg, qkvg, qkvg, qkvg, qkvg)


def _gla_attn_kernel(q_ref, k_ref, v_ref, g_ref, z_ref, wg_ref, bg_ref, o_ref, state_ref, *, chunk):
    @pl.when(pl.program_id(1) == 0)
    def _():
        state_ref[...] = jnp.zeros_like(state_ref)

    heads, dv, dk = state_ref.shape
    tile = q_ref.shape[0]
    shift = chunk.bit_length() - 1
    pre = jnp.dot(z_ref[...], wg_ref[...], preferred_element_type=F32) + bg_ref[...]
    log_a = (jnp.minimum(pre, 0.0) - jnp.log(1.0 + jnp.exp(-jnp.abs(pre)))) * (1.0 / GLA_GATE_TAU)
    ii = lax.broadcasted_iota(jnp.int32, (tile, tile), 0)
    jj = lax.broadcasted_iota(jnp.int32, (tile, tile), 1)
    same_chunk = (ii >> shift) == (jj >> shift)
    causal = jnp.logical_and(same_chunk, ii >= jj)
    sums = jnp.where(causal, 1.0, 0.0).astype(BF16)
    hi = log_a.astype(BF16)
    rest = log_a - hi.astype(F32)
    mid = rest.astype(BF16)
    lo = (rest - mid.astype(F32)).astype(BF16)
    b = (jnp.dot(sums, hi, preferred_element_type=F32) + jnp.dot(sums, mid, preferred_element_type=F32)
         + jnp.dot(sums, lo, preferred_element_type=F32))
    n_chunks = tile // chunk
    b_ends = [b[(c + 1) * chunk - 1:(c + 1) * chunk, :] for c in range(n_chunks)]
    b_last = jnp.concatenate([jnp.broadcast_to(e, (chunk, e.shape[1])) for e in b_ends], axis=0)

    k = k_ref[...].astype(F32)
    q_dec = (q_ref[...].astype(F32) * jnp.exp(b)).astype(BF16)
    k_dec = (k * jnp.exp(-b)).astype(BF16)
    k_carry = (k * jnp.exp(b_last - b)).astype(BF16)
    chunk_dec = [jnp.exp(e) for e in b_ends]

    kcs = [slice(h * dk, (h + 1) * dk) for h in range(heads)]
    vcs = [slice(h * dv, (h + 1) * dv) for h in range(heads)]
    probs = []
    for kc in kcs:
        scores = lax.dot_general(q_dec[:, kc], k_dec[:, kc], (((1,), (1,)), ((), ())), preferred_element_type=F32)
        probs.append(jnp.where(causal, scores, 0.0).astype(BF16))
    outs = [jnp.dot(p, v_ref[:, vc], preferred_element_type=F32) for p, vc in zip(probs, vcs)]
    states = [state_ref[h] for h in range(heads)]
    inter = [[] for _ in range(heads)]
    for c in range(tile // chunk):
        rows = slice(c * chunk, (c + 1) * chunk)
        for h, (kc, vc) in enumerate(zip(kcs, vcs)):
            inter[h].append(lax.dot_general(q_dec[rows, kc], states[h].astype(BF16), (((1,), (1,)), ((), ())),
                                            preferred_element_type=F32))
            update = lax.dot_general(v_ref[rows, vc], k_carry[rows, kc], (((0,), (0,)), ((), ())),
                                     preferred_element_type=F32)
            states[h] = chunk_dec[c][:, kc] * states[h] + update
    for h, vc in enumerate(vcs):
        state_ref[h] = states[h]
        o = outs[h] + jnp.concatenate(inter[h], axis=0)
        o_ref[:, vc] = (g_ref[:, vc].astype(F32) * _head_rms(o)).astype(o_ref.dtype)


def _gla_attn(proj, z, w_gate_up, b_gate, heads):
    b, s, n = proj.shape
    dk = n // (6 * heads)
    dv = 2 * dk
    tile = min(s, ROWS_GLA)
    return pl.pallas_call(
        functools.partial(_gla_attn_kernel, chunk=GLA_CHUNK),
        grid=(b, s // tile),
        in_specs=[
            pl.BlockSpec((None, tile, heads * dk), lambda i, j: (i, j, 0)),
            pl.BlockSpec((None, tile, heads * dk), lambda i, j: (i, j, 1)),
            pl.BlockSpec((None, tile, heads * dv), lambda i, j: (i, j, 1)),
            pl.BlockSpec((None, tile, heads * dv), lambda i, j: (i, j, 2)),
            pl.BlockSpec((None, tile, V7X_LANES), lambda i, j: (i, j, 0)),
            pl.BlockSpec((V7X_LANES, heads * dk), lambda i, j: (0, 0)),
            pl.BlockSpec((1, heads * dk), lambda i, j: (0, 0)),
        ],
        out_specs=pl.BlockSpec((None, tile, heads * dv), lambda i, j: (i, j, 0)),
        out_shape=jax.ShapeDtypeStruct((b, s, heads * dv), BF16),
        scratch_shapes=[pltpu.VMEM((heads, dv, dk), F32)],
        compiler_params=_params(
            ("parallel", "arbitrary"),
            _vmem_limit(tile * heads * (2 * dk + 3 * dv) * 2 + 2 * V7X_LANES * heads * dk * 2,
                        heads * dk * dv * 4, 24 * MIB),
        ),
        name="gla_core",
    )(proj, proj, proj, proj, z, w_gate_up, b_gate.reshape(1, -1))


def _conv_in_kernel(x_ref, sc_ref, sh_ref, g_ref, wb_ref, wc_ref, wu_ref, cw_ref, *rest, n_casts):
    cast_srcs, o_ref, cast_dsts = rest[:n_casts], rest[n_casts], rest[n_casts + 1:2 * n_casts + 1]
    h_ref, gs_ref, carry_ref = rest[2 * n_casts + 1:]
    j = pl.program_id(1)
    n = pl.program_id(2)
    tm, tn = o_ref.shape

    prev_tile = jnp.where(j > 0, carry_ref[n], 0.0)

    def conv_rows(r0, nrows, prev_rows):
        rows = slice(r0, r0 + nrows)
        h = h_ref[rows, :]
        row = lax.broadcasted_iota(jnp.int32, (nrows, 1), 0)
        last = {}
        for c in range(0, tn, 256):
            cols = slice(c, c + 256)
            b_gate = jnp.dot(h, wb_ref[:, cols], preferred_element_type=F32)
            c_gate = jnp.dot(h, wc_ref[:, cols], preferred_element_type=F32)
            u = c_gate * jnp.dot(h, wu_ref[:, cols], preferred_element_type=F32)
            prev = prev_rows[c]
            last[c] = u[nrows - V7X_SUBLANES:, :]
            u1 = jnp.where(row == 0, prev[7:8, :], pltpu.roll(u, 1, axis=0))
            u2 = jnp.where(row == 0, prev[6:7, :],
                           jnp.where(row == 1, prev[7:8, :], pltpu.roll(u, 2, axis=0)))
            y = cw_ref[0:1, cols] * u2 + cw_ref[1:2, cols] * u1 + cw_ref[2:3, cols] * u
            o_ref[rows, cols] = (b_gate * y).astype(o_ref.dtype)
        return last

    def store_carry(last):
        for c, rows8 in last.items():
            carry_ref[n, :, c:c + 256] = rows8

    tile_prev = {c: prev_tile[:, c:c + 256] for c in range(0, tn, 256)}

    @pl.when(n == 0)
    def _():
        gs_ref[...] = g_ref[...] * (1.0 + sc_ref[...])
        prev_rows = tile_prev
        group = min(PROLOGUE_ROWS, tm)
        for r in range(0, tm, group):
            for t in range(r, r + group, BF16_ROWS):
                _norm_mod_tile(x_ref, gs_ref, sh_ref, h_ref, slice(t, t + BF16_ROWS))
            prev_rows = conv_rows(r, group, prev_rows)
        store_carry(prev_rows)
        _cast_slabs(cast_srcs, cast_dsts)

    @pl.when(n > 0)
    def _():
        store_carry(conv_rows(0, tm, tile_prev))
        _cast_slabs(cast_srcs, cast_dsts)


def _conv_in(x, scale, shift, gain, w, conv_w, side_casts=()):
    b, s, d = x.shape
    tm = min(s, ROWS_CONV)
    tn = 512
    nb = d // tn
    nj = s // tm
    vec = pl.BlockSpec((None, 1, d), lambda i, j, n: (i, 0, 0))

    def w_spec(part):
        return pl.BlockSpec((d, tn), lambda i, j, n: (0, part * nb + n))

    c_in, c_out, c_shape, c_args, c_vmem = _side_casts(side_casts, b * nj * nb, lambda i, j, n: (i * nj + j) * nb + n)
    return pl.pallas_call(
        functools.partial(_conv_in_kernel, n_casts=len(side_casts)),
        grid=(b, nj, nb),
        in_specs=[
            pl.BlockSpec((None, tm, d), lambda i, j, n: (i, j, 0)),
            vec,
            vec,
            pl.BlockSpec((1, d), lambda i, j, n: (0, 0)),
            w_spec(0),
            w_spec(1),
            w_spec(2),
            pl.BlockSpec((CONV_WIDTH, tn), lambda i, j, n: (0, n)),
        ] + c_in,
        out_specs=[pl.BlockSpec((None, tm, tn), lambda i, j, n: (i, j, n))] + c_out,
        out_shape=[jax.ShapeDtypeStruct((b, s, d), BF16)] + c_shape,
        scratch_shapes=[pltpu.VMEM((tm, d), BF16), pltpu.VMEM((1, d), F32),
                        pltpu.VMEM((nb, V7X_SUBLANES, tn), F32)],
        compiler_params=_params(
            ("arbitrary", "arbitrary", "arbitrary"),
            _vmem_limit(tm * d * 4 + 3 * d * tn * 2 + tm * tn * 2, tm * d * 2 + c_vmem, 8 * tm * tn * 4),
        ),
        name="conv_in",
    )(x, scale, shift, gain.reshape(1, d), w, w, w, conv_w, *c_args)


def _pool_kernel(x_ref, sc_ref, sh_ref, g_ref, w_ref, ps_ref, gate_ref, o_ref, carry_ref):
    j = pl.program_id(1)
    tm, d = x_ref.shape
    halo = carry_ref.shape[0]
    group = d // len(POOL_WINDOWS)
    x = x_ref[...]
    h = _norm_mod(x, g_ref[...], sc_ref[...], sh_ref[...])
    prev = jnp.where(j > 0, carry_ref[...], 0.0)
    carry_ref[...] = h[tm - halo:, :]
    t = j * tm + lax.broadcasted_iota(jnp.int32, (tm, 1), 0)

    for gi, win in enumerate(POOL_WINDOWS):
        cols = slice(gi * group, (gi + 1) * group)
        hg = h[:, cols]
        ext = jnp.concatenate([prev[:, cols], hg], axis=0)
        k = 1
        while k < win:
            ext = ext + pltpu.roll(ext, k, axis=0)
            k *= 2
        count = jnp.minimum(t + 1, win).astype(F32)
        mixed = ext[halo:, :] / count - hg
        y = jnp.dot(mixed.astype(BF16), w_ref[gi], preferred_element_type=F32)
        o_ref[:, cols] = x[:, cols] + gate_ref[:, cols] * (y * ps_ref[:, cols])


def _pool_mixer(x, scale, shift, gain, pool_w, pool_scale, gate):
    b, s, d = x.shape
    groups, p, _ = pool_w.shape
    tm = min(s, ROWS_POOL)
    halo = 2 * V7X_SUBLANES
    vec = pl.BlockSpec((None, 1, d), lambda i, j: (i, 0, 0))
    row = pl.BlockSpec((1, d), lambda i, j: (0, 0))
    tile = pl.BlockSpec((None, tm, d), lambda i, j: (i, j, 0))
    return pl.pallas_call(
        _pool_kernel,
        grid=(b, s // tm),
        in_specs=[tile, vec, vec, row, pl.BlockSpec((groups, p, p), lambda i, j: (0, 0, 0)), row, vec],
        out_specs=tile,
        out_shape=jax.ShapeDtypeStruct((b, s, d), F32),
        scratch_shapes=[pltpu.VMEM((halo, d), F32)],
        compiler_params=_params(
            ("parallel", "arbitrary"),
            _vmem_limit(2 * tm * d * 4 + groups * p * p * 2, halo * d * 4, 6 * tm * d * 4),
        ),
        name="pool_mixer",
    )(x, scale, shift, gain.reshape(1, d), pool_w, pool_scale.reshape(1, d), gate)


def _ffn_kernel(x_ref, sc_ref, sh_ref, g_ref, gate_ref, wa_ref, wb_ref, wo_ref, fg_ref, *rest, final_norm, n_casts):
    cast_srcs, o_ref, cast_dsts = rest[:n_casts], rest[n_casts], rest[n_casts + 1:2 * n_casts + 1]
    h_ref, gs_ref = rest[2 * n_casts + 1:]
    f = pl.program_id(2)

    def ffn_part(rows):
        h = h_ref[rows, :]
        acts = []
        for c in range(0, wa_ref.shape[1], 256):
            a = jnp.dot(h, wa_ref[:, c:c + 256], preferred_element_type=F32)
            b = jnp.dot(h, wb_ref[:, c:c + 256], preferred_element_type=F32)
            acts.append((_silu(a) * b).astype(BF16))
        act = jnp.concatenate(acts, axis=-1)
        return gate_ref[...] * jnp.dot(act, wo_ref[...], preferred_element_type=F32)

    @pl.when(f == 0)
    def _():
        gs_ref[...] = g_ref[...] * (1.0 + sc_ref[...])
        group = min(PROLOGUE_ROWS, x_ref.shape[0])
        for r in range(0, x_ref.shape[0], group):
            rows = slice(r, r + group)
            for t in range(r, r + group, BF16_ROWS):
                _norm_mod_tile(x_ref, gs_ref, sh_ref, h_ref, slice(t, t + BF16_ROWS))
            o_ref[rows, :] = x_ref[rows, :] + ffn_part(rows)
        _cast_slabs(cast_srcs, cast_dsts)

    @pl.when(f > 0)
    def _():
        o_ref[...] += ffn_part(slice(None))
        _cast_slabs(cast_srcs, cast_dsts)

    if final_norm:
        @pl.when(f == pl.num_programs(2) - 1)
        def _():
            for r in range(0, o_ref.shape[0], 4 * V7X_SUBLANES):
                rows = slice(r, r + 4 * V7X_SUBLANES)
                o_ref[rows, :] = _head_rms(o_ref[rows, :]) * fg_ref[...]


def _ffn(x, scale, shift, gain, gate, w_in, w_out, final_gain, final_norm, side_casts=()):
    b, s, d = x.shape
    dff = w_out.shape[0]
    tm = min(s, ROWS_FFN)
    tf = 512
    nf = dff // tf
    nj = s // tm
    vec = pl.BlockSpec((None, 1, d), lambda i, j, f: (i, 0, 0))
    row = pl.BlockSpec((1, d), lambda i, j, f: (0, 0))
    tile = pl.BlockSpec((None, tm, d), lambda i, j, f: (i, j, 0))
    c_in, c_out, c_shape, c_args, c_vmem = _side_casts(side_casts, b * nj * nf, lambda i, j, f: (i * nj + j) * nf + f)
    return pl.pallas_call(
        functools.partial(_ffn_kernel, final_norm=final_norm, n_casts=len(side_casts)),
        grid=(b, nj, nf),
        in_specs=[
            tile,
            vec,
            vec,
            row,
            vec,
            pl.BlockSpec((d, tf), lambda i, j, f: (0, f)),
            pl.BlockSpec((d, tf), lambda i, j, f: (0, nf + f)),
            pl.BlockSpec((tf, d), lambda i, j, f: (f, 0)),
            row,
        ] + c_in,
        out_specs=[tile] + c_out,
        out_shape=[jax.ShapeDtypeStruct((b, s, d), F32)] + c_shape,
        scratch_shapes=[pltpu.VMEM((tm, d), BF16), pltpu.VMEM((1, d), F32)],
        compiler_params=_params(
            ("arbitrary", "arbitrary", "arbitrary"),
            _vmem_limit(2 * tm * d * 4 + 3 * d * tf * 2, tm * d * 2 + c_vmem, 4 * tm * 256 * 4),
        ),
        name="swiglu_ffn",
    )(x, scale, shift, gain.reshape(1, d), gate, w_in, w_in, w_out, final_gain.reshape(1, d), *c_args)


def kernel(x, c, positions, w_mod, b_mod, norm1_g, norm2_g, ret_w_in, ret_w_out, conv_w_in, conv_w, conv_w_out,
           gla_w_in, gla_w_gate_up, gla_b_gate, gla_w_out, pool_w, pool_scale, ffn_w_in, ffn_w_out, final_g):
    b, s, d = x.shape
    depth = w_mod.shape[0]
    assert depth >= 1 and d % (len(POOL_WINDOWS) * V7X_LANES) == 0 and s % GLA_CHUNK == 0

    mod = _modulation(c, w_mod, b_mod).reshape(depth, b, 6, 1, d)
    ret_dk = d // RET_HEADS
    cos, sin = _rope_tables(positions, ret_dk // 2)

    mixer_weights = {0: {"w_in": ret_w_in, "w_out": ret_w_out}, 1: {"w_in": conv_w_in, "w_out": conv_w_out},
                     2: {"w_in": gla_w_in, "w_out": gla_w_out}, 3: {}}
    ready = {}

    def pending(layer):
        return [((layer, name), stack, layer // N_MIXERS) for name, stack in mixer_weights[layer % N_MIXERS].items()]

    def take(key, stack, index):
        return ready.pop(key) if key in ready else stack[index].astype(BF16)

    def host(call, todo):
        outs = call(side_casts=tuple((stack, index) for _, stack, index in todo))
        for (key, _, _), w in zip(todo, outs[len(outs) - len(todo):]):
            ready[key] = w
        return outs[:len(outs) - len(todo)]

    lane_pad = V7X_LANES - GLA_GATE_RANK
    for i in range(depth):
        sh1, sc1, g1, sh2, sc2, g2 = (mod[i, :, p] for p in range(6))
        m, j = i % N_MIXERS, i // N_MIXERS
        w_in, w_out = (take((i, name), stack, j) for (_, name), stack, j in pending(i)) if m < 3 else (None, None)
        todo = pending(i + 1) if i + 1 < depth else []
        if i == 0:
            todo = todo + [((0, "ffn_in"), ffn_w_in, 0), ((0, "ffn_out"), ffn_w_out, 0)]
        if m == 0:
            qk_blocks = w_in.shape[1] // 6 // 1024
            col_blocks = ((qk_blocks, "rope", 1.0), (qk_blocks, "rope", ret_dk ** -0.5),
                          (2 * qk_blocks, "plain", 1.0), (2 * qk_blocks, "silu", 1.0))
            (qkvg,) = host(functools.partial(_proj_in, x, sc1, sh1, norm1_g[i], w_in, col_blocks, rope=(cos, sin)),
                           todo)
            x = _out_proj(_ret_attn(qkvg, RET_HEADS), w_out, x, g1)
        elif m == 1:
            (z,) = host(functools.partial(_conv_in, x, sc1, sh1, norm1_g[i], w_in, conv_w[j]), todo)
            x = _out_proj(z, w_out, x, g1)
        elif m == 2:
            gla_cols = w_in.shape[1] - GLA_GATE_RANK
            wz = jnp.pad(w_in[:, gla_cols:], ((0, 0), (0, lane_pad)))
            w_up = jnp.pad(gla_w_gate_up[j], ((0, lane_pad), (0, 0))).astype(BF16)
            qk_blocks = gla_cols // 6 // 1024
            col_blocks = ((qk_blocks, "plain", (gla_cols // 6 // GLA_HEADS) ** -0.5), (qk_blocks, "plain", 1.0),
                          (2 * qk_blocks, "plain", 1.0), (2 * qk_blocks, "silu", 1.0))
            proj, z = host(functools.partial(_proj_in, x, sc1, sh1, norm1_g[i], w_in, col_blocks, wz=wz), todo)
            x = _out_proj(_gla_attn(proj, z, w_up, gla_b_gate[j], GLA_HEADS), w_out, x, g1)
        else:
            x = _pool_mixer(x, sc1, sh1, norm1_g[i], pool_w[j].astype(BF16), pool_scale[j], g1)
            for key, stack, index in todo:
                ready[key] = stack[index].astype(BF16)
        ffn_in, ffn_out = take((i, "ffn_in"), ffn_w_in, i), take((i, "ffn_out"), ffn_w_out, i)
        todo = [((i + 1, "ffn_in"), ffn_w_in, i + 1), ((i + 1, "ffn_out"), ffn_w_out, i + 1)] if i + 1 < depth else []
        (x,) = host(functools.partial(_ffn, x, sc2, sh2, norm2_g[i], g2, ffn_in, ffn_out, final_g, i == depth - 1),
                    todo)
    return x
```

```python
import functools

import jax
import jax.numpy as jnp
from jax import lax
from jax.experimental import pallas as pl
from jax.experimental.pallas import tpu as pltpu

F32 = jnp.float32
BF16 = jnp.bfloat16

EPS = 1e-6
N_MIXERS = 4
RET_HEADS = 8
ROPE_BASE = 10000.0
CONV_WIDTH = 3
GLA_HEADS = 4
GLA_GATE_RANK = 16
GLA_GATE_TAU = 16.0
GLA_CHUNK = 64
POOL_WINDOWS = (2, 4, 8, 16)

V7X_VMEM_BYTES = 64 * 1024 * 1024
V7X_LANES = 128
V7X_SUBLANES = 8
BF16_ROWS = 2 * V7X_SUBLANES
MIB = 1024 * 1024

ROWS_ROPE = 1024
ROWS_PROJ = 1024
COLS_PROJ = 2048
ROWS_OUT_PROJ = 512
ROWS_RET = 1024
RET_CHUNK = 256
ROWS_GLA = 256
ROWS_CONV = 1024
ROWS_POOL = 512
ROWS_FFN = 1024
PROLOGUE_ROWS = 256


def _vmem_limit(pipelined_block_bytes, scratch_bytes, temp_bytes):
    need = 2 * pipelined_block_bytes + scratch_bytes + temp_bytes + 2 * MIB
    return int(min(need, V7X_VMEM_BYTES - 6 * MIB))


def _params(semantics, vmem_bytes):
    return pltpu.CompilerParams(dimension_semantics=semantics, vmem_limit_bytes=vmem_bytes)


def _norm_mod(x, gain, scale, shift):
    ms = jnp.mean(x * x, axis=-1, keepdims=True)
    return (x * lax.rsqrt(ms + EPS)) * gain * (1.0 + scale) + shift


def _norm_mod_tile(x_ref, gs_ref, sh_ref, h_ref, rows):
    x = x_ref[rows, :]
    ms = jnp.mean(x * x, axis=-1, keepdims=True)
    h_ref[rows, :] = ((x * lax.rsqrt(ms + EPS)) * gs_ref[...] + sh_ref[...]).astype(h_ref.dtype)


def _cast_plan(rows, cols, steps):
    options = []
    for size in range(BF16_ROWS, rows + 1, BF16_ROWS):
        if rows % size == 0 and rows // size <= steps:
            options.append((size * cols, 0, size, rows // size))
            break
    for size in range(V7X_LANES, cols + 1, V7X_LANES):
        if cols % size == 0 and cols // size <= steps:
            options.append((rows * size, 1, size, cols // size))
            break
    assert options, "weight cannot be cast in slabs within this grid"
    _, axis, size, count = min(options)
    return axis, size, count


def _side_casts(side_casts, steps, step_of):
    in_specs, out_specs, out_shapes, args, vmem = [], [], [], [], 0
    for stack, layer in side_casts:
        _, rows, cols = stack.shape
        axis, size, count = _cast_plan(rows, cols, steps)

        def slab(*ids, count=count):
            return jnp.minimum(step_of(*ids), count - 1)

        if axis == 0:
            in_specs.append(pl.BlockSpec((None, size, cols), lambda *ids, la=layer, sl=slab: (la, sl(*ids), 0)))
            out_specs.append(pl.BlockSpec((size, cols), lambda *ids, sl=slab: (sl(*ids), 0)))
            vmem += 2 * size * cols * 6
        else:
            in_specs.append(pl.BlockSpec((None, rows, size), lambda *ids, la=layer, sl=slab: (la, 0, sl(*ids))))
            out_specs.append(pl.BlockSpec((rows, size), lambda *ids, sl=slab: (0, sl(*ids))))
            vmem += 2 * rows * size * 6
        out_shapes.append(jax.ShapeDtypeStruct((rows, cols), BF16))
        args.append(stack)
    return in_specs, out_specs, out_shapes, args, vmem


def _cast_slabs(src_refs, dst_refs):
    for src, dst in zip(src_refs, dst_refs):
        dst[...] = src[...].astype(BF16)


def _silu(x):
    return x * jax.nn.sigmoid(x)


def _head_rms(o):
    return o * lax.rsqrt(jnp.mean(o * o, axis=-1, keepdims=True) + EPS)


def _mod_kernel(c_ref, w_ref, b_ref, o_ref):
    c_act = _silu(c_ref[...]).astype(BF16)
    o_ref[...] = jnp.dot(c_act, w_ref[...].astype(BF16), preferred_element_type=F32) + b_ref[...]


def _modulation(c, w_mod, b_mod):
    depth, d, n = w_mod.shape
    b = c.shape[0]
    tn = 1024
    return pl.pallas_call(
        _mod_kernel,
        grid=(depth, n // tn),
        in_specs=[
            pl.BlockSpec((b, d), lambda l, j: (0, 0)),
            pl.BlockSpec((None, d, tn), lambda l, j: (l, 0, j)),
            pl.BlockSpec((None, 1, tn), lambda l, j: (l, 0, j)),
        ],
        out_specs=pl.BlockSpec((None, b, tn), lambda l, j: (l, 0, j)),
        out_shape=jax.ShapeDtypeStruct((depth, b, n), F32),
        compiler_params=_params(("parallel", "parallel"), _vmem_limit(d * tn * 4, 0, d * tn * 2)),
        name="adaln_mod",
    )(c, w_mod, b_mod.reshape(depth, 1, n))


def _rope_kernel(pos_ref, invf_ref, cos_ref, sin_ref):
    ang = pos_ref[...].astype(F32) * invf_ref[...]
    cos_ref[...] = jnp.cos(ang)
    sin_ref[...] = jnp.sin(ang)


def _rope_tables(positions, half):
    b, s = positions.shape
    inv_freq = jnp.power(ROPE_BASE, -jnp.linspace(0.0, 1.0, half, dtype=F32)).reshape(1, half)
    ts = min(s, ROWS_ROPE)
    spec = pl.BlockSpec((None, ts, half), lambda i, j: (i, j, 0))
    return pl.pallas_call(
        _rope_kernel,
        grid=(b, s // ts),
        in_specs=[pl.BlockSpec((None, ts, 1), lambda i, j: (i, j, 0)), pl.BlockSpec((1, half), lambda i, j: (0, 0))],
        out_specs=[spec, spec],
        out_shape=[jax.ShapeDtypeStruct((b, s, half), F32)] * 2,
        compiler_params=_params(("parallel", "parallel"), _vmem_limit(3 * ts * V7X_LANES * 4, 0, 8 * MIB)),
        name="rope_tables",
    )(positions.reshape(b, s, 1), inv_freq)


def _rope_apply(t, cos, sin):
    half = cos.shape[-1]
    parts = []
    for c in range(0, t.shape[-1], 2 * half):
        t1, t2 = t[:, c:c + half], t[:, c + half:c + 2 * half]
        parts += [t1 * cos - t2 * sin, t2 * cos + t1 * sin]
    return jnp.concatenate(parts, axis=-1)


def _proj_in_kernel(*refs, col_steps, with_rope, with_z, n_casts):
    x_ref, sc_ref, sh_ref, g_ref, w_ref = refs[:5]
    rest = list(refs[5:])
    cos_ref, sin_ref = (rest.pop(0), rest.pop(0)) if with_rope else (None, None)
    wz_ref = rest.pop(0) if with_z else None
    cast_srcs = [rest.pop(0) for _ in range(n_casts)]
    o_ref = rest.pop(0)
    z_ref = rest.pop(0) if with_z else None
    cast_dsts = [rest.pop(0) for _ in range(n_casts)]
    h_ref, gs_ref = rest
    k = pl.program_id(2)

    def emit(posts, rows=slice(None)):
        y = jnp.dot(h_ref[rows, :], w_ref[...], preferred_element_type=F32)
        width = y.shape[1] // len(posts)
        for p, (kind, scale) in enumerate(posts):
            cols = slice(p * width, (p + 1) * width)
            part = y[:, cols]
            if kind == "rope":
                part = _rope_apply(part, cos_ref[rows, :], sin_ref[rows, :])
            elif kind == "silu":
                part = _silu(part)
            if scale != 1.0:
                part = part * scale
            o_ref[rows, cols] = part.astype(o_ref.dtype)

    def emit_step(posts):
        emit(posts)
        _cast_slabs(cast_srcs, cast_dsts)

    @pl.when(k == 0)
    def _():
        gs_ref[...] = g_ref[...] * (1.0 + sc_ref[...])
        tm = x_ref.shape[0]
        group = min(PROLOGUE_ROWS, tm)
        for r in range(0, tm, group):
            for t in range(r, r + group, BF16_ROWS):
                _norm_mod_tile(x_ref, gs_ref, sh_ref, h_ref, slice(t, t + BF16_ROWS))
            emit(col_steps[0][1], slice(r, r + group))
        if with_z:
            z_ref[...] = jnp.dot(h_ref[...], wz_ref[...], preferred_element_type=F32).astype(z_ref.dtype)
        _cast_slabs(cast_srcs, cast_dsts)

    start = 0
    for n_steps, posts in col_steps:
        pl.when(jnp.logical_and(k >= max(start, 1), k < start + n_steps))(functools.partial(emit_step, posts))
        start += n_steps


def _proj_in(x, scale, shift, gain, w, col_steps, rope=None, wz=None, side_casts=()):
    b, s, d = x.shape
    tm = min(s, ROWS_PROJ)
    tn = COLS_PROJ
    nj = s // tm
    nk = sum(n for n, _ in col_steps)
    n_cols = tn * nk
    with_z = wz is not None
    with_rope = rope is not None
    vec = pl.BlockSpec((None, 1, d), lambda i, j, k: (i, 0, 0))
    in_specs = [
        pl.BlockSpec((None, tm, d), lambda i, j, k: (i, j, 0)),
        vec,
        vec,
        pl.BlockSpec((1, d), lambda i, j, k: (0, 0)),
        pl.BlockSpec((d, tn), lambda i, j, k: (0, k)),
    ]
    out_specs = [pl.BlockSpec((None, tm, tn), lambda i, j, k: (i, j, k))]
    out_shape = [jax.ShapeDtypeStruct((b, s, n_cols), BF16)]
    args = [x, scale, shift, gain.reshape(1, d), w]
    if with_rope:
        half = rope[0].shape[-1]
        in_specs += [pl.BlockSpec((None, tm, half), lambda i, j, k: (i, j, 0))] * 2
        args += list(rope)
    if with_z:
        in_specs.append(pl.BlockSpec((d, V7X_LANES), lambda i, j, k: (0, 0)))
        out_specs.append(pl.BlockSpec((None, tm, V7X_LANES), lambda i, j, k: (i, j, 0)))
        out_shape.append(jax.ShapeDtypeStruct((b, s, V7X_LANES), BF16))
        args.append(wz)
    c_in, c_out, c_shape, c_args, c_vmem = _side_casts(side_casts, b * nj * nk, lambda i, j, k: (i * nj + j) * nk + k)
    outs = pl.pallas_call(
        functools.partial(_proj_in_kernel, col_steps=col_steps, with_rope=with_rope, with_z=with_z,
                          n_casts=len(side_casts)),
        grid=(b, nj, nk),
        in_specs=in_specs + c_in,
        out_specs=out_specs + c_out,
        out_shape=out_shape + c_shape,
        scratch_shapes=[pltpu.VMEM((tm, d), BF16), pltpu.VMEM((1, d), F32)],
        compiler_params=_params(
            ("arbitrary", "arbitrary", "arbitrary"),
            _vmem_limit(tm * d * 4 + d * tn * 2 + tm * tn * 2 + (d + 5 * tm) * V7X_LANES * 2, tm * d * 2 + c_vmem,
                        2 * tm * tn * 4),
        ),
        name="proj_in",
    )(*args, *c_args)
    return list(outs)


def _out_proj_kernel(a_ref, w_ref, x_ref, gate_ref, o_ref):
    y = jnp.dot(a_ref[...], w_ref[...], preferred_element_type=F32)
    o_ref[...] = x_ref[...] + gate_ref[...] * y


def _out_proj(a, w, x, gate):
    b, s, k = a.shape
    d = w.shape[1]
    tm = min(s, ROWS_OUT_PROJ)
    tile = pl.BlockSpec((None, tm, d), lambda i, j: (i, j, 0))
    return pl.pallas_call(
        _out_proj_kernel,
        grid=(b, s // tm),
        in_specs=[
            pl.BlockSpec((None, tm, k), lambda i, j: (i, j, 0)),
            pl.BlockSpec((k, d), lambda i, j: (0, 0), pipeline_mode=pl.Buffered(1)),
            tile,
            pl.BlockSpec((None, 1, d), lambda i, j: (i, 0, 0)),
        ],
        out_specs=tile,
        out_shape=jax.ShapeDtypeStruct((b, s, d), F32),
        compiler_params=_params(
            ("parallel", "parallel"),
            _vmem_limit(tm * k * 2 + 2 * tm * d * 4, k * d * 2, tm * d * 4),
        ),
        name="out_proj",
    )(a, w, x, gate)


def _ret_attn_kernel(lg_ref, q_ref, k_ref, v_ref, g_ref, o_ref, state_ref, *, chunk):
    @pl.when(pl.program_id(2) == 0)
    def _():
        state_ref[...] = jnp.zeros_like(state_ref)

    tile = q_ref.shape[0]
    lg = lg_ref[:, 0:1]
    ii = lax.broadcasted_iota(jnp.int32, (chunk, chunk), 0)
    jj = lax.broadcasted_iota(jnp.int32, (chunk, chunk), 1)
    decay = jnp.where(ii >= jj, jnp.exp((ii - jj).astype(F32) * lg), 0.0)
    r = lax.broadcasted_iota(jnp.int32, (chunk, 1), 0).astype(F32)
    row_dec = jnp.exp((r + 1.0) * lg)
    col_dec = jnp.exp((chunk - 1.0 - r) * lg)
    chunk_dec = jnp.exp(chunk * lg)

    sls = [pl.ds(c * chunk, chunk) for c in range(tile // chunk)]
    probs = []
    for sl in sls:
        scores = lax.dot_general(q_ref[sl, :], k_ref[sl, :], (((1,), (1,)), ((), ())), preferred_element_type=F32)
        probs.append((scores * decay).astype(BF16))
    outs = [jnp.dot(p, v_ref[sl, :], preferred_element_type=F32) for p, sl in zip(probs, sls)]
    updates = [lax.dot_general((k_ref[sl, :].astype(F32) * col_dec).astype(BF16), v_ref[sl, :],
                               (((0,), (0,)), ((), ())), preferred_element_type=F32) for sl in sls]
    state = state_ref[...]
    for c, sl in enumerate(sls):
        q_dec = (q_ref[sl, :].astype(F32) * row_dec).astype(BF16)
        outs[c] = outs[c] + jnp.dot(q_dec, state.astype(BF16), preferred_element_type=F32)
        state = chunk_dec * state + updates[c]
    state_ref[...] = state
    for c, sl in enumerate(sls):
        o_ref[sl, :] = (g_ref[sl, :].astype(F32) * _head_rms(outs[c])).astype(o_ref.dtype)


def _ret_attn(qkvg, heads):
    b, s, n = qkvg.shape
    dk = n // (6 * heads)
    dv = 2 * dk
    tile = min(s, ROWS_RET)
    chunk = min(tile, RET_CHUNK)
    log_gamma = jnp.log1p(-jnp.exp2(-5.0 - jnp.arange(heads, dtype=F32)))
    lg = jnp.broadcast_to(log_gamma[:, None, None], (heads, 1, V7X_LANES))
    return pl.pallas_call(
        functools.partial(_ret_attn_kernel, chunk=chunk),
        grid=(b, heads, s // tile),
        in_specs=[
            pl.BlockSpec((None, 1, V7X_LANES), lambda i, h, j: (h, 0, 0)),
            pl.BlockSpec((None, tile, dk), lambda i, h, j: (i, j, h)),
            pl.BlockSpec((None, tile, dk), lambda i, h, j: (i, j, heads + h)),
            pl.BlockSpec((None, tile, dv), lambda i, h, j: (i, j, heads + h)),
            pl.BlockSpec((None, tile, dv), lambda i, h, j: (i, j, 2 * heads + h)),
        ],
        out_specs=pl.BlockSpec((None, tile, dv), lambda i, h, j: (i, j, h)),
        out_shape=jax.ShapeDtypeStruct((b, s, heads * dv), BF16),
        scratch_shapes=[pltpu.VMEM((dk, dv), F32)],
        compiler_params=_params(
            ("parallel", "parallel", "arbitrary"),
            _vmem_limit(tile * (2 * dk + 3 * dv) * 2, dk * dv * 4, 16 * MIB),
        ),
        name="retention_core",
    )(lg, qkvg, qkvg, qkvg, qkvg)


def _gla_attn_kernel(q_ref, k_ref, v_ref, g_ref, z_ref, wg_ref, bg_ref, o_ref, state_ref, *, chunk):
    @pl.when(pl.program_id(1) == 0)
    def _():
        state_ref[...] = jnp.zeros_like(state_ref)

    heads, dv, dk = state_ref.shape
    tile = q_ref.shape[0]
    shift = chunk.bit_length() - 1
    pre = jnp.dot(z_ref[...], wg_ref[...], preferred_element_type=F32) + bg_ref[...]
    log_a = (jnp.minimum(pre, 0.0) - jnp.log(1.0 + jnp.exp(-jnp.abs(pre)))) * (1.0 / GLA_GATE_TAU)
    ii = lax.broadcasted_iota(jnp.int32, (tile, tile), 0)
    jj = lax.broadcasted_iota(jnp.int32, (tile, tile), 1)
    same_chunk = (ii >> shift) == (jj >> shift)
    causal = jnp.logical_and(same_chunk, ii >= jj)
    sums = jnp.where(causal, 1.0, 0.0).astype(BF16)
    hi = log_a.astype(BF16)
    rest = log_a - hi.astype(F32)
    mid = rest.astype(BF16)
    lo = (rest - mid.astype(F32)).astype(BF16)
    b = (jnp.dot(sums, hi, preferred_element_type=F32) + jnp.dot(sums, mid, preferred_element_type=F32)
         + jnp.dot(sums, lo, preferred_element_type=F32))
    n_chunks = tile // chunk
    b_ends = [b[(c + 1) * chunk - 1:(c + 1) * chunk, :] for c in range(n_chunks)]
    b_last = jnp.concatenate([jnp.broadcast_to(e, (chunk, e.shape[1])) for e in b_ends], axis=0)

    k = k_ref[...].astype(F32)
    q_dec = (q_ref[...].astype(F32) * jnp.exp(b)).astype(BF16)
    k_dec = (k * jnp.exp(-b)).astype(BF16)
    k_carry = (k * jnp.exp(b_last - b)).astype(BF16)
    chunk_dec = [jnp.exp(e) for e in b_ends]

    kcs = [slice(h * dk, (h + 1) * dk) for h in range(heads)]
    vcs = [slice(h * dv, (h + 1) * dv) for h in range(heads)]
    probs = []
    for kc in kcs:
        scores = lax.dot_general(q_dec[:, kc], k_dec[:, kc], (((1,), (1,)), ((), ())), preferred_element_type=F32)
        probs.append(jnp.where(causal, scores, 0.0).astype(BF16))
    outs = [jnp.dot(p, v_ref[:, vc], preferred_element_type=F32) for p, vc in zip(probs, vcs)]
    states = [state_ref[h] for h in range(heads)]
    inter = [[] for _ in range(heads)]
    for c in range(tile // chunk):
        rows = slice(c * chunk, (c + 1) * chunk)
        for h, (kc, vc) in enumerate(zip(kcs, vcs)):
            inter[h].append(lax.dot_general(q_dec[rows, kc], states[h].astype(BF16), (((1,), (1,)), ((), ())),
                                            preferred_element_type=F32))
            update = lax.dot_general(v_ref[rows, vc], k_carry[rows, kc], (((0,), (0,)), ((), ())),
                                     preferred_element_type=F32)
            states[h] = chunk_dec[c][:, kc] * states[h] + update
    for h, vc in enumerate(vcs):
        state_ref[h] = states[h]
        o = outs[h] + jnp.concatenate(inter[h], axis=0)
        o_ref[:, vc] = (g_ref[:, vc].astype(F32) * _head_rms(o)).astype(o_ref.dtype)


def _gla_attn(proj, z, w_gate_up, b_gate, heads):
    b, s, n = proj.shape
    dk = n // (6 * heads)
    dv = 2 * dk
    tile = min(s, ROWS_GLA)
    return pl.pallas_call(
        functools.partial(_gla_attn_kernel, chunk=GLA_CHUNK),
        grid=(b, s // tile),
        in_specs=[
            pl.BlockSpec((None, tile, heads * dk), lambda i, j: (i, j, 0)),
            pl.BlockSpec((None, tile, heads * dk), lambda i, j: (i, j, 1)),
            pl.BlockSpec((None, tile, heads * dv), lambda i, j: (i, j, 1)),
            pl.BlockSpec((None, tile, heads * dv), lambda i, j: (i, j, 2)),
            pl.BlockSpec((None, tile, V7X_LANES), lambda i, j: (i, j, 0)),
            pl.BlockSpec((V7X_LANES, heads * dk), lambda i, j: (0, 0)),
            pl.BlockSpec((1, heads * dk), lambda i, j: (0, 0)),
        ],
        out_specs=pl.BlockSpec((None, tile, heads * dv), lambda i, j: (i, j, 0)),
        out_shape=jax.ShapeDtypeStruct((b, s, heads * dv), BF16),
        scratch_shapes=[pltpu.VMEM((heads, dv, dk), F32)],
        compiler_params=_params(
            ("parallel", "arbitrary"),
            _vmem_limit(tile * heads * (2 * dk + 3 * dv) * 2 + 2 * V7X_LANES * heads * dk * 2,
                        heads * dk * dv * 4, 24 * MIB),
        ),
        name="gla_core",
    )(proj, proj, proj, proj, z, w_gate_up, b_gate.reshape(1, -1))


def _conv_in_kernel(x_ref, sc_ref, sh_ref, g_ref, wb_ref, wc_ref, wu_ref, cw_ref, *rest, n_casts):
    cast_srcs, o_ref, cast_dsts = rest[:n_casts], rest[n_casts], rest[n_casts + 1:2 * n_casts + 1]
    h_ref, gs_ref, carry_ref = rest[2 * n_casts + 1:]
    j = pl.program_id(1)
    n = pl.program_id(2)
    tm, tn = o_ref.shape

    prev_tile = jnp.where(j > 0, carry_ref[n], 0.0)

    def conv_rows(r0, nrows, prev_rows):
        rows = slice(r0, r0 + nrows)
        h = h_ref[rows, :]
        row = lax.broadcasted_iota(jnp.int32, (nrows, 1), 0)
        last = {}
        for c in range(0, tn, 256):
            cols = slice(c, c + 256)
            b_gate = jnp.dot(h, wb_ref[:, cols], preferred_element_type=F32)
            c_gate = jnp.dot(h, wc_ref[:, cols], preferred_element_type=F32)
            u = c_gate * jnp.dot(h, wu_ref[:, cols], preferred_element_type=F32)
            prev = prev_rows[c]
            last[c] = u[nrows - V7X_SUBLANES:, :]
            u1 = jnp.where(row == 0, prev[7:8, :], pltpu.roll(u, 1, axis=0))
            u2 = jnp.where(row == 0, prev[6:7, :],
                           jnp.where(row == 1, prev[7:8, :], pltpu.roll(u, 2, axis=0)))
            y = cw_ref[0:1, cols] * u2 + cw_ref[1:2, cols] * u1 + cw_ref[2:3, cols] * u
            o_ref[rows, cols] = (b_gate * y).astype(o_ref.dtype)
        return last

    def store_carry(last):
        for c, rows8 in last.items():
            carry_ref[n, :, c:c + 256] = rows8

    tile_prev = {c: prev_tile[:, c:c + 256] for c in range(0, tn, 256)}

    @pl.when(n == 0)
    def _():
        gs_ref[...] = g_ref[...] * (1.0 + sc_ref[...])
        prev_rows = tile_prev
        group = min(PROLOGUE_ROWS, tm)
        for r in range(0, tm, group):
            for t in range(r, r + group, BF16_ROWS):
                _norm_mod_tile(x_ref, gs_ref, sh_ref, h_ref, slice(t, t + BF16_ROWS))
            prev_rows = conv_rows(r, group, prev_rows)
        store_carry(prev_rows)
        _cast_slabs(cast_srcs, cast_dsts)

    @pl.when(n > 0)
    def _():
        store_carry(conv_rows(0, tm, tile_prev))
        _cast_slabs(cast_srcs, cast_dsts)


def _conv_in(x, scale, shift, gain, w, conv_w, side_casts=()):
    b, s, d = x.shape
    tm = min(s, ROWS_CONV)
    tn = 1024
    nb = d // tn
    nj = s // tm
    vec = pl.BlockSpec((None, 1, d), lambda i, j, n: (i, 0, 0))

    def w_spec(part):
        return pl.BlockSpec((d, tn), lambda i, j, n: (0, part * nb + n))

    c_in, c_out, c_shape, c_args, c_vmem = _side_casts(side_casts, b * nj * nb, lambda i, j, n: (i * nj + j) * nb + n)
    return pl.pallas_call(
        functools.partial(_conv_in_kernel, n_casts=len(side_casts)),
        grid=(b, nj, nb),
        in_specs=[
            pl.BlockSpec((None, tm, d), lambda i, j, n: (i, j, 0)),
            vec,
            vec,
            pl.BlockSpec((1, d), lambda i, j, n: (0, 0)),
            w_spec(0),
            w_spec(1),
            w_spec(2),
            pl.BlockSpec((CONV_WIDTH, tn), lambda i, j, n: (0, n)),
        ] + c_in,
        out_specs=[pl.BlockSpec((None, tm, tn), lambda i, j, n: (i, j, n))] + c_out,
        out_shape=[jax.ShapeDtypeStruct((b, s, d), BF16)] + c_shape,
        scratch_shapes=[pltpu.VMEM((tm, d), BF16), pltpu.VMEM((1, d), F32),
                        pltpu.VMEM((nb, V7X_SUBLANES, tn), F32)],
        compiler_params=_params(
            ("arbitrary", "arbitrary", "arbitrary"),
            _vmem_limit(tm * d * 4 + 3 * d * tn * 2 + tm * tn * 2, tm * d * 2 + c_vmem, 8 * tm * tn * 4),
        ),
        name="conv_in",
    )(x, scale, shift, gain.reshape(1, d), w, w, w, conv_w, *c_args)


def _pool_kernel(x_ref, sc_ref, sh_ref, g_ref, w_ref, ps_ref, gate_ref, o_ref, carry_ref):
    j = pl.program_id(1)
    tm, d = x_ref.shape
    halo = carry_ref.shape[0]
    group = d // len(POOL_WINDOWS)
    x = x_ref[...]
    h = _norm_mod(x, g_ref[...], sc_ref[...], sh_ref[...])
    prev = jnp.where(j > 0, carry_ref[...], 0.0)
    carry_ref[...] = h[tm - halo:, :]
    t = j * tm + lax.broadcasted_iota(jnp.int32, (tm, 1), 0)

    for gi, win in enumerate(POOL_WINDOWS):
        cols = slice(gi * group, (gi + 1) * group)
        hg = h[:, cols]
        ext = jnp.concatenate([prev[:, cols], hg], axis=0)
        k = 1
        while k < win:
            ext = ext + pltpu.roll(ext, k, axis=0)
            k *= 2
        inv_count = 1.0 / jnp.minimum(t + 1, win).astype(F32)
        mixed = ext[halo:, :] * inv_count - hg
        y = jnp.dot(mixed.astype(BF16), w_ref[gi], preferred_element_type=F32)
        o_ref[:, cols] = x[:, cols] + gate_ref[:, cols] * (y * ps_ref[:, cols])


def _pool_mixer(x, scale, shift, gain, pool_w, pool_scale, gate):
    b, s, d = x.shape
    groups, p, _ = pool_w.shape
    tm = min(s, ROWS_POOL)
    halo = 2 * V7X_SUBLANES
    vec = pl.BlockSpec((None, 1, d), lambda i, j: (i, 0, 0))
    row = pl.BlockSpec((1, d), lambda i, j: (0, 0))
    tile = pl.BlockSpec((None, tm, d), lambda i, j: (i, j, 0))
    return pl.pallas_call(
        _pool_kernel,
        grid=(b, s // tm),
        in_specs=[tile, vec, vec, row, pl.BlockSpec((groups, p, p), lambda i, j: (0, 0, 0)), row, vec],
        out_specs=tile,
        out_shape=jax.ShapeDtypeStruct((b, s, d), F32),
        scratch_shapes=[pltpu.VMEM((halo, d), F32)],
        compiler_params=_params(
            ("parallel", "arbitrary"),
            _vmem_limit(2 * tm * d * 4 + groups * p * p * 2, halo * d * 4, 6 * tm * d * 4),
        ),
        name="pool_mixer",
    )(x, scale, shift, gain.reshape(1, d), pool_w, pool_scale.reshape(1, d), gate)


def _ffn_kernel(x_ref, sc_ref, sh_ref, g_ref, gate_ref, wa_ref, wb_ref, wo_ref, fg_ref, *rest, final_norm, n_casts):
    cast_srcs, o_ref, cast_dsts = rest[:n_casts], rest[n_casts], rest[n_casts + 1:2 * n_casts + 1]
    h_ref, gs_ref = rest[2 * n_casts + 1:]
    f = pl.program_id(2)

    def ffn_part(rows):
        h = h_ref[rows, :]
        acts = []
        for c in range(0, wa_ref.shape[1], 256):
            a = jnp.dot(h, wa_ref[:, c:c + 256], preferred_element_type=F32)
            b = jnp.dot(h, wb_ref[:, c:c + 256], preferred_element_type=F32)
            acts.append((_silu(a) * b).astype(BF16))
        act = jnp.concatenate(acts, axis=-1)
        return gate_ref[...] * jnp.dot(act, wo_ref[...], preferred_element_type=F32)

    @pl.when(f == 0)
    def _():
        gs_ref[...] = g_ref[...] * (1.0 + sc_ref[...])
        group = min(PROLOGUE_ROWS, x_ref.shape[0])
        for r in range(0, x_ref.shape[0], group):
            rows = slice(r, r + group)
            for t in range(r, r + group, BF16_ROWS):
                _norm_mod_tile(x_ref, gs_ref, sh_ref, h_ref, slice(t, t + BF16_ROWS))
            o_ref[rows, :] = x_ref[rows, :] + ffn_part(rows)
        _cast_slabs(cast_srcs, cast_dsts)

    @pl.when(f > 0)
    def _():
        o_ref[...] += ffn_part(slice(None))
        _cast_slabs(cast_srcs, cast_dsts)

    if final_norm:
        @pl.when(f == pl.num_programs(2) - 1)
        def _():
            for r in range(0, o_ref.shape[0], 4 * V7X_SUBLANES):
                rows = slice(r, r + 4 * V7X_SUBLANES)
                o_ref[rows, :] = _head_rms(o_ref[rows, :]) * fg_ref[...]


def _ffn(x, scale, shift, gain, gate, w_in, w_out, final_gain, final_norm, side_casts=()):
    b, s, d = x.shape
    dff = w_out.shape[0]
    tm = min(s, ROWS_FFN)
    tf = 512
    nf = dff // tf
    nj = s // tm
    vec = pl.BlockSpec((None, 1, d), lambda i, j, f: (i, 0, 0))
    row = pl.BlockSpec((1, d), lambda i, j, f: (0, 0))
    tile = pl.BlockSpec((None, tm, d), lambda i, j, f: (i, j, 0))
    c_in, c_out, c_shape, c_args, c_vmem = _side_casts(side_casts, b * nj * nf, lambda i, j, f: (i * nj + j) * nf + f)
    return pl.pallas_call(
        functools.partial(_ffn_kernel, final_norm=final_norm, n_casts=len(side_casts)),
        grid=(b, nj, nf),
        in_specs=[
            tile,
            vec,
            vec,
            row,
            vec,
            pl.BlockSpec((d, tf), lambda i, j, f: (0, f)),
            pl.BlockSpec((d, tf), lambda i, j, f: (0, nf + f)),
            pl.BlockSpec((tf, d), lambda i, j, f: (f, 0)),
            row,
        ] + c_in,
        out_specs=[tile] + c_out,
        out_shape=[jax.ShapeDtypeStruct((b, s, d), F32)] + c_shape,
        scratch_shapes=[pltpu.VMEM((tm, d), BF16), pltpu.VMEM((1, d), F32)],
        compiler_params=_params(
            ("arbitrary", "arbitrary", "arbitrary"),
            _vmem_limit(2 * tm * d * 4 + 3 * d * tf * 2, tm * d * 2 + c_vmem, 4 * tm * 256 * 4),
        ),
        name="swiglu_ffn",
    )(x, scale, shift, gain.reshape(1, d), gate, w_in, w_in, w_out, final_gain.reshape(1, d), *c_args)


def kernel(x, c, positions, w_mod, b_mod, norm1_g, norm2_g, ret_w_in, ret_w_out, conv_w_in, conv_w, conv_w_out,
           gla_w_in, gla_w_gate_up, gla_b_gate, gla_w_out, pool_w, pool_scale, ffn_w_in, ffn_w_out, final_g):
    b, s, d = x.shape
    depth = w_mod.shape[0]
    assert depth >= 1 and d % (len(POOL_WINDOWS) * V7X_LANES) == 0 and s % GLA_CHUNK == 0

    mod = _modulation(c, w_mod, b_mod).reshape(depth, b, 6, 1, d)
    ret_dk = d // RET_HEADS
    cos, sin = _rope_tables(positions, ret_dk // 2)

    mixer_weights = {0: {"w_in": ret_w_in, "w_out": ret_w_out}, 1: {"w_in": conv_w_in, "w_out": conv_w_out},
                     2: {"w_in": gla_w_in, "w_out": gla_w_out}, 3: {}}
    ready = {}

    def pending(layer):
        return [((layer, name), stack, layer // N_MIXERS) for name, stack in mixer_weights[layer % N_MIXERS].items()]

    def hostable(todo):
        return [t for t in todo if t[1].shape[-1] % V7X_LANES == 0]

    def take(key, stack, index):
        return ready.pop(key) if key in ready else stack[index].astype(BF16)

    def host(call, todo):
        outs = call(side_casts=tuple((stack, index) for _, stack, index in todo))
        for (key, _, _), w in zip(todo, outs[len(outs) - len(todo):]):
            ready[key] = w
        return outs[:len(outs) - len(todo)]

    lane_pad = V7X_LANES - GLA_GATE_RANK
    for i in range(depth):
        sh1, sc1, g1, sh2, sc2, g2 = (mod[i, :, p] for p in range(6))
        m, j = i % N_MIXERS, i // N_MIXERS
        w_in, w_out = (take((i, name), stack, j) for (_, name), stack, j in pending(i)) if m < 3 else (None, None)
        todo = hostable(pending(i + 1)) if i + 1 < depth else []
        if i == 0:
            todo = todo + [((0, "ffn_in"), ffn_w_in, 0), ((0, "ffn_out"), ffn_w_out, 0)]
        if m == 0:
            assert w_in.shape[1] == 6 * COLS_PROJ
            col_steps = ((1, (("rope", 1.0),)), (1, (("rope", ret_dk ** -0.5),)),
                         (2, (("plain", 1.0),)), (2, (("silu", 1.0),)))
            (qkvg,) = host(functools.partial(_proj_in, x, sc1, sh1, norm1_g[i], w_in, col_steps, rope=(cos, sin)),
                           todo)
            x = _out_proj(_ret_attn(qkvg, RET_HEADS), w_out, x, g1)
        elif m == 1:
            (z,) = host(functools.partial(_conv_in, x, sc1, sh1, norm1_g[i], w_in, conv_w[j]), todo)
            x = _out_proj(z, w_out, x, g1)
        elif m == 2:
            gla_cols = w_in.shape[1] - GLA_GATE_RANK
            wz = jnp.pad(w_in[:, gla_cols:], ((0, 0), (0, lane_pad)))
            w_up = jnp.pad(gla_w_gate_up[j], ((0, lane_pad), (0, 0))).astype(BF16)
            assert gla_cols == 3 * COLS_PROJ
            col_steps = ((1, (("plain", (gla_cols // 6 // GLA_HEADS) ** -0.5), ("plain", 1.0))),
                         (1, (("plain", 1.0),)), (1, (("silu", 1.0),)))
            proj, z = host(functools.partial(_proj_in, x, sc1, sh1, norm1_g[i], w_in, col_steps, wz=wz), todo)
            x = _out_proj(_gla_attn(proj, z, w_up, gla_b_gate[j], GLA_HEADS), w_out, x, g1)
        else:
            x = _pool_mixer(x, sc1, sh1, norm1_g[i], pool_w[j].astype(BF16), pool_scale[j], g1)
            for key, stack, index in todo:
                ready[key] = stack[index].astype(BF16)
        ffn_in, ffn_out = take((i, "ffn_in"), ffn_w_in, i), take((i, "ffn_out"), ffn_w_out, i)
        todo = [((i + 1, "ffn_in"), ffn_w_in, i + 1), ((i + 1, "ffn_out"), ffn_w_out, i + 1)] if i + 1 < depth else []
        (x,) = host(functools.partial(_ffn, x, sc2, sh2, norm2_g[i], g2, ffn_in, ffn_out, final_g, i == depth - 1),
                    todo)
    return x
```

```python
import functools

import jax
import jax.numpy as jnp
from jax import lax
from jax.experimental import pallas as pl
from jax.experimental.pallas import tpu as pltpu

F32 = jnp.float32
BF16 = jnp.bfloat16

EPS = 1e-6
N_MIXERS = 4
RET_HEADS = 8
ROPE_BASE = 10000.0
CONV_WIDTH = 3
GLA_HEADS = 4
GLA_GATE_RANK = 16
GLA_GATE_TAU = 16.0
GLA_CHUNK = 64
POOL_WINDOWS = (2, 4, 8, 16)

V7X_VMEM_BYTES = 64 * 1024 * 1024
V7X_LANES = 128
V7X_SUBLANES = 8
BF16_ROWS = 2 * V7X_SUBLANES
MIB = 1024 * 1024

ROWS_ROPE = 1024
ROWS_PROJ = 1024
COLS_PROJ = 2048
ROWS_OUT_PROJ = 512
ROWS_RET = 1024
RET_CHUNK = 256
RET_HEADS_PER_STEP = 4
ROWS_GLA = 256
ROWS_CONV = 1024
ROWS_POOL = 512
ROWS_FFN = 1024
PROLOGUE_ROWS = 256


def _vmem_limit(pipelined_block_bytes, scratch_bytes, temp_bytes):
    need = 2 * pipelined_block_bytes + scratch_bytes + temp_bytes + 2 * MIB
    return int(min(need, V7X_VMEM_BYTES - 6 * MIB))


def _params(semantics, vmem_bytes):
    return pltpu.CompilerParams(dimension_semantics=semantics, vmem_limit_bytes=vmem_bytes)


def _norm_mod(x, gain, scale, shift):
    ms = jnp.mean(x * x, axis=-1, keepdims=True)
    return (x * lax.rsqrt(ms + EPS)) * gain * (1.0 + scale) + shift


def _norm_mod_tile(x_ref, gs_ref, sh_ref, h_ref, rows):
    x = x_ref[rows, :]
    ms = jnp.mean(x * x, axis=-1, keepdims=True)
    h_ref[rows, :] = ((x * lax.rsqrt(ms + EPS)) * gs_ref[...] + sh_ref[...]).astype(h_ref.dtype)


def _cast_plan(rows, cols, steps):
    options = []
    for size in range(BF16_ROWS, rows + 1, BF16_ROWS):
        if rows % size == 0 and rows // size <= steps:
            options.append((size * cols, 0, size, rows // size))
            break
    for size in range(V7X_LANES, cols + 1, V7X_LANES):
        if cols % size == 0 and cols // size <= steps:
            options.append((rows * size, 1, size, cols // size))
            break
    assert options, "weight cannot be cast in slabs within this grid"
    _, axis, size, count = min(options)
    return axis, size, count


def _side_casts(side_casts, steps, step_of):
    in_specs, out_specs, out_shapes, args, vmem = [], [], [], [], 0
    for stack, layer in side_casts:
        _, rows, cols = stack.shape
        axis, size, count = _cast_plan(rows, cols, steps)

        def slab(*ids, count=count):
            return jnp.minimum(step_of(*ids), count - 1)

        if axis == 0:
            in_specs.append(pl.BlockSpec((None, size, cols), lambda *ids, la=layer, sl=slab: (la, sl(*ids), 0)))
            out_specs.append(pl.BlockSpec((size, cols), lambda *ids, sl=slab: (sl(*ids), 0)))
            vmem += 2 * size * cols * 6
        else:
            in_specs.append(pl.BlockSpec((None, rows, size), lambda *ids, la=layer, sl=slab: (la, 0, sl(*ids))))
            out_specs.append(pl.BlockSpec((rows, size), lambda *ids, sl=slab: (0, sl(*ids))))
            vmem += 2 * rows * size * 6
        out_shapes.append(jax.ShapeDtypeStruct((rows, cols), BF16))
        args.append(stack)
    return in_specs, out_specs, out_shapes, args, vmem


def _cast_slabs(src_refs, dst_refs):
    for src, dst in zip(src_refs, dst_refs):
        dst[...] = src[...].astype(BF16)


def _silu(x):
    return x * jax.nn.sigmoid(x)


def _head_rms(o):
    return o * lax.rsqrt(jnp.mean(o * o, axis=-1, keepdims=True) + EPS)


def _mod_kernel(c_ref, w_ref, b_ref, o_ref):
    c_act = _silu(c_ref[...]).astype(BF16)
    o_ref[...] = jnp.dot(c_act, w_ref[...].astype(BF16), preferred_element_type=F32) + b_ref[...]


def _modulation(c, w_mod, b_mod):
    depth, d, n = w_mod.shape
    b = c.shape[0]
    tn = 1024
    return pl.pallas_call(
        _mod_kernel,
        grid=(depth, n // tn),
        in_specs=[
            pl.BlockSpec((b, d), lambda l, j: (0, 0)),
            pl.BlockSpec((None, d, tn), lambda l, j: (l, 0, j)),
            pl.BlockSpec((None, 1, tn), lambda l, j: (l, 0, j)),
        ],
        out_specs=pl.BlockSpec((None, b, tn), lambda l, j: (l, 0, j)),
        out_shape=jax.ShapeDtypeStruct((depth, b, n), F32),
        compiler_params=_params(("parallel", "parallel"), _vmem_limit(d * tn * 4, 0, d * tn * 2)),
        name="adaln_mod",
    )(c, w_mod, b_mod.reshape(depth, 1, n))


def _rope_kernel(pos_ref, invf_ref, *rest, n_casts):
    cast_srcs, (cos_ref, sin_ref), cast_dsts = rest[:n_casts], rest[n_casts:n_casts + 2], rest[n_casts + 2:]
    ang = pos_ref[...].astype(F32) * invf_ref[...]
    cos_ref[...] = jnp.cos(ang)
    sin_ref[...] = jnp.sin(ang)
    _cast_slabs(cast_srcs, cast_dsts)


def _rope_tables(positions, half, side_casts=()):
    b, s = positions.shape
    inv_freq = jnp.power(ROPE_BASE, -jnp.linspace(0.0, 1.0, half, dtype=F32)).reshape(1, half)
    ts = min(s, ROWS_ROPE)
    nj = s // ts
    spec = pl.BlockSpec((None, ts, half), lambda i, j: (i, j, 0))
    c_in, c_out, c_shape, c_args, c_vmem = _side_casts(side_casts, b * nj, lambda i, j: i * nj + j)
    return pl.pallas_call(
        functools.partial(_rope_kernel, n_casts=len(side_casts)),
        grid=(b, nj),
        in_specs=[pl.BlockSpec((None, ts, 1), lambda i, j: (i, j, 0)),
                  pl.BlockSpec((1, half), lambda i, j: (0, 0))] + c_in,
        out_specs=[spec, spec] + c_out,
        out_shape=[jax.ShapeDtypeStruct((b, s, half), F32)] * 2 + c_shape,
        compiler_params=_params(("arbitrary", "arbitrary"), _vmem_limit(3 * ts * V7X_LANES * 4, c_vmem, 8 * MIB)),
        name="rope_tables",
    )(positions.reshape(b, s, 1), inv_freq, *c_args)


def _rope_apply(t, cos, sin):
    half = cos.shape[-1]
    parts = []
    for c in range(0, t.shape[-1], 2 * half):
        t1, t2 = t[:, c:c + half], t[:, c + half:c + 2 * half]
        parts += [t1 * cos - t2 * sin, t2 * cos + t1 * sin]
    return jnp.concatenate(parts, axis=-1)


def _proj_in_kernel(*refs, col_steps, with_rope, with_z, n_casts):
    x_ref, sc_ref, sh_ref, g_ref, w_ref = refs[:5]
    rest = list(refs[5:])
    cos_ref, sin_ref = (rest.pop(0), rest.pop(0)) if with_rope else (None, None)
    wz_ref = rest.pop(0) if with_z else None
    cast_srcs = [rest.pop(0) for _ in range(n_casts)]
    o_ref = rest.pop(0)
    z_ref = rest.pop(0) if with_z else None
    cast_dsts = [rest.pop(0) for _ in range(n_casts)]
    h_ref, gs_ref = rest
    k = pl.program_id(2)

    def emit(posts, rows=slice(None)):
        y = jnp.dot(h_ref[rows, :], w_ref[...], preferred_element_type=F32)
        width = y.shape[1] // len(posts)
        for p, (kind, scale) in enumerate(posts):
            cols = slice(p * width, (p + 1) * width)
            part = y[:, cols]
            if kind == "rope":
                part = _rope_apply(part, cos_ref[rows, :], sin_ref[rows, :])
            elif kind == "silu":
                part = _silu(part)
            if scale != 1.0:
                part = part * scale
            o_ref[rows, cols] = part.astype(o_ref.dtype)

    def emit_step(posts):
        emit(posts)
        _cast_slabs(cast_srcs, cast_dsts)

    @pl.when(k == 0)
    def _():
        gs_ref[...] = g_ref[...] * (1.0 + sc_ref[...])
        tm = x_ref.shape[0]
        group = min(PROLOGUE_ROWS, tm)
        for r in range(0, tm, group):
            for t in range(r, r + group, BF16_ROWS):
                _norm_mod_tile(x_ref, gs_ref, sh_ref, h_ref, slice(t, t + BF16_ROWS))
            emit(col_steps[0][1], slice(r, r + group))
        if with_z:
            z_ref[...] = jnp.dot(h_ref[...], wz_ref[...], preferred_element_type=F32).astype(z_ref.dtype)
        _cast_slabs(cast_srcs, cast_dsts)

    start = 0
    for n_steps, posts in col_steps:
        pl.when(jnp.logical_and(k >= max(start, 1), k < start + n_steps))(functools.partial(emit_step, posts))
        start += n_steps


def _proj_in(x, scale, shift, gain, w, col_steps, rope=None, wz=None, side_casts=()):
    b, s, d = x.shape
    tm = min(s, ROWS_PROJ)
    tn = COLS_PROJ
    nj = s // tm
    nk = sum(n for n, _ in col_steps)
    n_cols = tn * nk
    with_z = wz is not None
    with_rope = rope is not None
    vec = pl.BlockSpec((None, 1, d), lambda i, j, k: (i, 0, 0))
    in_specs = [
        pl.BlockSpec((None, tm, d), lambda i, j, k: (i, j, 0)),
        vec,
        vec,
        pl.BlockSpec((1, d), lambda i, j, k: (0, 0)),
        pl.BlockSpec((d, tn), lambda i, j, k: (0, k)),
    ]
    out_specs = [pl.BlockSpec((None, tm, tn), lambda i, j, k: (i, j, k))]
    out_shape = [jax.ShapeDtypeStruct((b, s, n_cols), BF16)]
    args = [x, scale, shift, gain.reshape(1, d), w]
    if with_rope:
        half = rope[0].shape[-1]
        in_specs += [pl.BlockSpec((None, tm, half), lambda i, j, k: (i, j, 0))] * 2
        args += list(rope)
    if with_z:
        in_specs.append(pl.BlockSpec((d, V7X_LANES), lambda i, j, k: (0, 0)))
        out_specs.append(pl.BlockSpec((None, tm, V7X_LANES), lambda i, j, k: (i, j, 0)))
        out_shape.append(jax.ShapeDtypeStruct((b, s, V7X_LANES), BF16))
        args.append(wz)
    c_in, c_out, c_shape, c_args, c_vmem = _side_casts(side_casts, b * nj * nk, lambda i, j, k: (i * nj + j) * nk + k)
    outs = pl.pallas_call(
        functools.partial(_proj_in_kernel, col_steps=col_steps, with_rope=with_rope, with_z=with_z,
                          n_casts=len(side_casts)),
        grid=(b, nj, nk),
        in_specs=in_specs + c_in,
        out_specs=out_specs + c_out,
        out_shape=out_shape + c_shape,
        scratch_shapes=[pltpu.VMEM((tm, d), BF16), pltpu.VMEM((1, d), F32)],
        compiler_params=_params(
            ("arbitrary", "arbitrary", "arbitrary"),
            _vmem_limit(tm * d * 4 + d * tn * 2 + tm * tn * 2 + (d + 5 * tm) * V7X_LANES * 2, tm * d * 2 + c_vmem,
                        2 * tm * tn * 4),
        ),
        name="proj_in",
    )(*args, *c_args)
    return list(outs)


def _out_proj_kernel(a_ref, w_ref, x_ref, gate_ref, o_ref):
    y = jnp.dot(a_ref[...], w_ref[...], preferred_element_type=F32)
    o_ref[...] = x_ref[...] + gate_ref[...] * y


def _out_proj(a, w, x, gate):
    b, s, k = a.shape
    d = w.shape[1]
    tm = min(s, ROWS_OUT_PROJ)
    tile = pl.BlockSpec((None, tm, d), lambda i, j: (i, j, 0))
    return pl.pallas_call(
        _out_proj_kernel,
        grid=(b, s // tm),
        in_specs=[
            pl.BlockSpec((None, tm, k), lambda i, j: (i, j, 0)),
            pl.BlockSpec((k, d), lambda i, j: (0, 0), pipeline_mode=pl.Buffered(1)),
            tile,
            pl.BlockSpec((None, 1, d), lambda i, j: (i, 0, 0)),
        ],
        out_specs=tile,
        out_shape=jax.ShapeDtypeStruct((b, s, d), F32),
        compiler_params=_params(
            ("parallel", "parallel"),
            _vmem_limit(tm * k * 2 + 2 * tm * d * 4, k * d * 2, tm * d * 4),
        ),
        name="out_proj",
    )(a, w, x, gate)


def _ret_attn_kernel(lg_ref, q_ref, k_ref, v_ref, g_ref, o_ref, state_ref, *, chunk):
    @pl.when(pl.program_id(2) == 0)
    def _():
        state_ref[...] = jnp.zeros_like(state_ref)

    heads, dk, dv = state_ref.shape
    tile = q_ref.shape[0]
    ii = lax.broadcasted_iota(jnp.int32, (chunk, chunk), 0)
    jj = lax.broadcasted_iota(jnp.int32, (chunk, chunk), 1)
    r = lax.broadcasted_iota(jnp.int32, (chunk, 1), 0).astype(F32)
    sls = [pl.ds(c * chunk, chunk) for c in range(tile // chunk)]

    for h in range(heads):
        kc = slice(h * dk, (h + 1) * dk)
        vc = slice(h * dv, (h + 1) * dv)
        lg = lg_ref[h][:, 0:1]
        decay = jnp.where(ii >= jj, jnp.exp((ii - jj).astype(F32) * lg), 0.0)
        row_dec = jnp.exp((r + 1.0) * lg)
        col_dec = jnp.exp((chunk - 1.0 - r) * lg)
        chunk_dec = jnp.exp(chunk * lg)
        probs = []
        for sl in sls:
            scores = lax.dot_general(q_ref[sl, kc], k_ref[sl, kc], (((1,), (1,)), ((), ())),
                                     preferred_element_type=F32)
            probs.append((scores * decay).astype(BF16))
        outs = [jnp.dot(p, v_ref[sl, vc], preferred_element_type=F32) for p, sl in zip(probs, sls)]
        updates = [lax.dot_general((k_ref[sl, kc].astype(F32) * col_dec).astype(BF16), v_ref[sl, vc],
                                   (((0,), (0,)), ((), ())), preferred_element_type=F32) for sl in sls]
        state = state_ref[h]
        for c, sl in enumerate(sls):
            q_dec = (q_ref[sl, kc].astype(F32) * row_dec).astype(BF16)
            outs[c] = outs[c] + jnp.dot(q_dec, state.astype(BF16), preferred_element_type=F32)
            state = chunk_dec * state + updates[c]
        state_ref[h] = state
        for c, sl in enumerate(sls):
            o_ref[sl, vc] = (g_ref[sl, vc].astype(F32) * _head_rms(outs[c])).astype(o_ref.dtype)


def _ret_attn(qkvg, heads):
    b, s, n = qkvg.shape
    dk = n // (6 * heads)
    dv = 2 * dk
    tile = min(s, ROWS_RET)
    chunk = min(tile, RET_CHUNK)
    hs = RET_HEADS_PER_STEP
    groups = heads // hs
    log_gamma = jnp.log1p(-jnp.exp2(-5.0 - jnp.arange(heads, dtype=F32)))
    lg = jnp.broadcast_to(log_gamma[:, None, None], (heads, 1, V7X_LANES))
    return pl.pallas_call(
        functools.partial(_ret_attn_kernel, chunk=chunk),
        grid=(b, groups, s // tile),
        in_specs=[
            pl.BlockSpec((hs, 1, V7X_LANES), lambda i, h, j: (h, 0, 0)),
            pl.BlockSpec((None, tile, hs * dk), lambda i, h, j: (i, j, h)),
            pl.BlockSpec((None, tile, hs * dk), lambda i, h, j: (i, j, groups + h)),
            pl.BlockSpec((None, tile, hs * dv), lambda i, h, j: (i, j, groups + h)),
            pl.BlockSpec((None, tile, hs * dv), lambda i, h, j: (i, j, 2 * groups + h)),
        ],
        out_specs=pl.BlockSpec((None, tile, hs * dv), lambda i, h, j: (i, j, h)),
        out_shape=jax.ShapeDtypeStruct((b, s, heads * dv), BF16),
        scratch_shapes=[pltpu.VMEM((hs, dk, dv), F32)],
        compiler_params=_params(
            ("parallel", "parallel", "arbitrary"),
            _vmem_limit(tile * hs * (2 * dk + 3 * dv) * 2, hs * dk * dv * 4, 16 * MIB),
        ),
        name="retention_core",
    )(lg, qkvg, qkvg, qkvg, qkvg)


def _gla_attn_kernel(q_ref, k_ref, v_ref, g_ref, z_ref, wg_ref, bg_ref, o_ref, state_ref, *, chunk):
    @pl.when(pl.program_id(1) == 0)
    def _():
        state_ref[...] = jnp.zeros_like(state_ref)

    heads, dv, dk = state_ref.shape
    tile = q_ref.shape[0]
    shift = chunk.bit_length() - 1
    pre = jnp.dot(z_ref[...], wg_ref[...], preferred_element_type=F32) + bg_ref[...]
    log_a = (jnp.minimum(pre, 0.0) - jnp.log(1.0 + jnp.exp(-jnp.abs(pre)))) * (1.0 / GLA_GATE_TAU)
    ii = lax.broadcasted_iota(jnp.int32, (tile, tile), 0)
    jj = lax.broadcasted_iota(jnp.int32, (tile, tile), 1)
    same_chunk = (ii >> shift) == (jj >> shift)
    causal = jnp.logical_and(same_chunk, ii >= jj)
    sums = jnp.where(causal, 1.0, 0.0).astype(BF16)
    hi = log_a.astype(BF16)
    rest = log_a - hi.astype(F32)
    mid = rest.astype(BF16)
    lo = (rest - mid.astype(F32)).astype(BF16)
    b = (jnp.dot(sums, hi, preferred_element_type=F32) + jnp.dot(sums, mid, preferred_element_type=F32)
         + jnp.dot(sums, lo, preferred_element_type=F32))
    n_chunks = tile // chunk
    b_ends = [b[(c + 1) * chunk - 1:(c + 1) * chunk, :] for c in range(n_chunks)]
    b_last = jnp.concatenate([jnp.broadcast_to(e, (chunk, e.shape[1])) for e in b_ends], axis=0)

    k = k_ref[...].astype(F32)
    q_dec = (q_ref[...].astype(F32) * jnp.exp(b)).astype(BF16)
    k_dec = (k * jnp.exp(-b)).astype(BF16)
    k_carry = (k * jnp.exp(b_last - b)).astype(BF16)
    chunk_dec = [jnp.exp(e) for e in b_ends]

    kcs = [slice(h * dk, (h + 1) * dk) for h in range(heads)]
    vcs = [slice(h * dv, (h + 1) * dv) for h in range(heads)]
    probs = []
    for kc in kcs:
        scores = lax.dot_general(q_dec[:, kc], k_dec[:, kc], (((1,), (1,)), ((), ())), preferred_element_type=F32)
        probs.append(jnp.where(causal, scores, 0.0).astype(BF16))
    outs = [jnp.dot(p, v_ref[:, vc], preferred_element_type=F32) for p, vc in zip(probs, vcs)]
    states = [state_ref[h] for h in range(heads)]
    inter = [[] for _ in range(heads)]
    for c in range(tile // chunk):
        rows = slice(c * chunk, (c + 1) * chunk)
        for h, (kc, vc) in enumerate(zip(kcs, vcs)):
            inter[h].append(lax.dot_general(q_dec[rows, kc], states[h].astype(BF16), (((1,), (1,)), ((), ())),
                                            preferred_element_type=F32))
            update = lax.dot_general(v_ref[rows, vc], k_carry[rows, kc], (((0,), (0,)), ((), ())),
                                     preferred_element_type=F32)
            states[h] = chunk_dec[c][:, kc] * states[h] + update
    for h, vc in enumerate(vcs):
        state_ref[h] = states[h]
        o = outs[h] + jnp.concatenate(inter[h], axis=0)
        o_ref[:, vc] = (g_ref[:, vc].astype(F32) * _head_rms(o)).astype(o_ref.dtype)


def _gla_attn(proj, z, w_gate_up, b_gate, heads):
    b, s, n = proj.shape
    dk = n // (6 * heads)
    dv = 2 * dk
    tile = min(s, ROWS_GLA)
    return pl.pallas_call(
        functools.partial(_gla_attn_kernel, chunk=GLA_CHUNK),
        grid=(b, s // tile),
        in_specs=[
            pl.BlockSpec((None, tile, heads * dk), lambda i, j: (i, j, 0)),
            pl.BlockSpec((None, tile, heads * dk), lambda i, j: (i, j, 1)),
            pl.BlockSpec((None, tile, heads * dv), lambda i, j: (i, j, 1)),
            pl.BlockSpec((None, tile, heads * dv), lambda i, j: (i, j, 2)),
            pl.BlockSpec((None, tile, V7X_LANES), lambda i, j: (i, j, 0)),
            pl.BlockSpec((V7X_LANES, heads * dk), lambda i, j: (0, 0)),
            pl.BlockSpec((1, heads * dk), lambda i, j: (0, 0)),
        ],
        out_specs=pl.BlockSpec((None, tile, heads * dv), lambda i, j: (i, j, 0)),
        out_shape=jax.ShapeDtypeStruct((b, s, heads * dv), BF16),
        scratch_shapes=[pltpu.VMEM((heads, dv, dk), F32)],
        compiler_params=_params(
            ("parallel", "arbitrary"),
            _vmem_limit(tile * heads * (2 * dk + 3 * dv) * 2 + 2 * V7X_LANES * heads * dk * 2,
                        heads * dk * dv * 4, 24 * MIB),
        ),
        name="gla_core",
    )(proj, proj, proj, proj, z, w_gate_up, b_gate.reshape(1, -1))


def _conv_in_kernel(x_ref, sc_ref, sh_ref, g_ref, wb_ref, wc_ref, wu_ref, cw_ref, *rest, n_casts):
    cast_srcs, o_ref, cast_dsts = rest[:n_casts], rest[n_casts], rest[n_casts + 1:2 * n_casts + 1]
    h_ref, gs_ref, carry_ref = rest[2 * n_casts + 1:]
    j = pl.program_id(1)
    n = pl.program_id(2)
    tm, tn = o_ref.shape

    prev_tile = jnp.where(j > 0, carry_ref[n], 0.0)

    def conv_rows(r0, nrows, prev_rows):
        rows = slice(r0, r0 + nrows)
        h = h_ref[rows, :]
        row = lax.broadcasted_iota(jnp.int32, (nrows, 1), 0)
        last = {}
        for c in range(0, tn, 256):
            cols = slice(c, c + 256)
            b_gate = jnp.dot(h, wb_ref[:, cols], preferred_element_type=F32)
            c_gate = jnp.dot(h, wc_ref[:, cols], preferred_element_type=F32)
            u = c_gate * jnp.dot(h, wu_ref[:, cols], preferred_element_type=F32)
            prev = prev_rows[c]
            last[c] = u[nrows - V7X_SUBLANES:, :]
            u1 = jnp.where(row == 0, prev[7:8, :], pltpu.roll(u, 1, axis=0))
            u2 = jnp.where(row == 0, prev[6:7, :],
                           jnp.where(row == 1, prev[7:8, :], pltpu.roll(u, 2, axis=0)))
            y = cw_ref[0:1, cols] * u2 + cw_ref[1:2, cols] * u1 + cw_ref[2:3, cols] * u
            o_ref[rows, cols] = (b_gate * y).astype(o_ref.dtype)
        return last

    def store_carry(last):
        for c, rows8 in last.items():
            carry_ref[n, :, c:c + 256] = rows8

    tile_prev = {c: prev_tile[:, c:c + 256] for c in range(0, tn, 256)}

    @pl.when(n == 0)
    def _():
        gs_ref[...] = g_ref[...] * (1.0 + sc_ref[...])
        prev_rows = tile_prev
        group = min(PROLOGUE_ROWS, tm)
        for r in range(0, tm, group):
            for t in range(r, r + group, BF16_ROWS):
                _norm_mod_tile(x_ref, gs_ref, sh_ref, h_ref, slice(t, t + BF16_ROWS))
            prev_rows = conv_rows(r, group, prev_rows)
        store_carry(prev_rows)
        _cast_slabs(cast_srcs, cast_dsts)

    @pl.when(n > 0)
    def _():
        store_carry(conv_rows(0, tm, tile_prev))
        _cast_slabs(cast_srcs, cast_dsts)


def _conv_in(x, scale, shift, gain, w, conv_w, side_casts=()):
    b, s, d = x.shape
    tm = min(s, ROWS_CONV)
    tn = 1024
    nb = d // tn
    nj = s // tm
    vec = pl.BlockSpec((None, 1, d), lambda i, j, n: (i, 0, 0))

    def w_spec(part):
        return pl.BlockSpec((d, tn), lambda i, j, n: (0, part * nb + n))

    c_in, c_out, c_shape, c_args, c_vmem = _side_casts(side_casts, b * nj * nb, lambda i, j, n: (i * nj + j) * nb + n)
    return pl.pallas_call(
        functools.partial(_conv_in_kernel, n_casts=len(side_casts)),
        grid=(b, nj, nb),
        in_specs=[
            pl.BlockSpec((None, tm, d), lambda i, j, n: (i, j, 0)),
            vec,
            vec,
            pl.BlockSpec((1, d), lambda i, j, n: (0, 0)),
            w_spec(0),
            w_spec(1),
            w_spec(2),
            pl.BlockSpec((CONV_WIDTH, tn), lambda i, j, n: (0, n)),
        ] + c_in,
        out_specs=[pl.BlockSpec((None, tm, tn), lambda i, j, n: (i, j, n))] + c_out,
        out_shape=[jax.ShapeDtypeStruct((b, s, d), BF16)] + c_shape,
        scratch_shapes=[pltpu.VMEM((tm, d), BF16), pltpu.VMEM((1, d), F32),
                        pltpu.VMEM((nb, V7X_SUBLANES, tn), F32)],
        compiler_params=_params(
            ("arbitrary", "arbitrary", "arbitrary"),
            _vmem_limit(tm * d * 4 + 3 * d * tn * 2 + tm * tn * 2, tm * d * 2 + c_vmem, 8 * tm * tn * 4),
        ),
        name="conv_in",
    )(x, scale, shift, gain.reshape(1, d), w, w, w, conv_w, *c_args)


def _pool_kernel(x_ref, sc_ref, sh_ref, g_ref, w_ref, ps_ref, gate_ref, o_ref, carry_ref):
    j = pl.program_id(1)
    tm, d = x_ref.shape
    halo = carry_ref.shape[0]
    group = d // len(POOL_WINDOWS)
    x = x_ref[...]
    h = _norm_mod(x, g_ref[...], sc_ref[...], sh_ref[...])
    prev = jnp.where(j > 0, carry_ref[...], 0.0)
    carry_ref[...] = h[tm - halo:, :]
    t = j * tm + lax.broadcasted_iota(jnp.int32, (tm, 1), 0)

    for gi, win in enumerate(POOL_WINDOWS):
        cols = slice(gi * group, (gi + 1) * group)
        hg = h[:, cols]
        ext = jnp.concatenate([prev[:, cols], hg], axis=0)
        k = 1
        while k < win:
            ext = ext + pltpu.roll(ext, k, axis=0)
            k *= 2
        inv_count = 1.0 / jnp.minimum(t + 1, win).astype(F32)
        mixed = ext[halo:, :] * inv_count - hg
        y = jnp.dot(mixed.astype(BF16), w_ref[gi], preferred_element_type=F32)
        o_ref[:, cols] = x[:, cols] + gate_ref[:, cols] * (y * ps_ref[:, cols])


def _pool_mixer(x, scale, shift, gain, pool_w, pool_scale, gate):
    b, s, d = x.shape
    groups, p, _ = pool_w.shape
    tm = min(s, ROWS_POOL)
    halo = 2 * V7X_SUBLANES
    vec = pl.BlockSpec((None, 1, d), lambda i, j: (i, 0, 0))
    row = pl.BlockSpec((1, d), lambda i, j: (0, 0))
    tile = pl.BlockSpec((None, tm, d), lambda i, j: (i, j, 0))
    return pl.pallas_call(
        _pool_kernel,
        grid=(b, s // tm),
        in_specs=[tile, vec, vec, row, pl.BlockSpec((groups, p, p), lambda i, j: (0, 0, 0)), row, vec],
        out_specs=tile,
        out_shape=jax.ShapeDtypeStruct((b, s, d), F32),
        scratch_shapes=[pltpu.VMEM((halo, d), F32)],
        compiler_params=_params(
            ("parallel", "arbitrary"),
            _vmem_limit(2 * tm * d * 4 + groups * p * p * 2, halo * d * 4, 6 * tm * d * 4),
        ),
        name="pool_mixer",
    )(x, scale, shift, gain.reshape(1, d), pool_w, pool_scale.reshape(1, d), gate)


def _ffn_kernel(x_ref, sc_ref, sh_ref, g_ref, gate_ref, wa_ref, wb_ref, wo_ref, fg_ref, *rest, final_norm, n_casts):
    cast_srcs, o_ref, cast_dsts = rest[:n_casts], rest[n_casts], rest[n_casts + 1:2 * n_casts + 1]
    h_ref, gs_ref = rest[2 * n_casts + 1:]
    f = pl.program_id(2)

    def ffn_part(rows):
        h = h_ref[rows, :]
        acts = []
        for c in range(0, wa_ref.shape[1], 256):
            a = jnp.dot(h, wa_ref[:, c:c + 256], preferred_element_type=F32)
            b = jnp.dot(h, wb_ref[:, c:c + 256], preferred_element_type=F32)
            acts.append((_silu(a) * b).astype(BF16))
        act = jnp.concatenate(acts, axis=-1)
        return gate_ref[...] * jnp.dot(act, wo_ref[...], preferred_element_type=F32)

    @pl.when(f == 0)
    def _():
        gs_ref[...] = g_ref[...] * (1.0 + sc_ref[...])
        group = min(PROLOGUE_ROWS, x_ref.shape[0])
        for r in range(0, x_ref.shape[0], group):
            rows = slice(r, r + group)
            for t in range(r, r + group, BF16_ROWS):
                _norm_mod_tile(x_ref, gs_ref, sh_ref, h_ref, slice(t, t + BF16_ROWS))
            o_ref[rows, :] = x_ref[rows, :] + ffn_part(rows)
        _cast_slabs(cast_srcs, cast_dsts)

    @pl.when(f > 0)
    def _():
        o_ref[...] += ffn_part(slice(None))
        _cast_slabs(cast_srcs, cast_dsts)

    if final_norm:
        @pl.when(f == pl.num_programs(2) - 1)
        def _():
            for r in range(0, o_ref.shape[0], 4 * V7X_SUBLANES):
                rows = slice(r, r + 4 * V7X_SUBLANES)
                o_ref[rows, :] = _head_rms(o_ref[rows, :]) * fg_ref[...]


def _ffn(x, scale, shift, gain, gate, w_in, w_out, final_gain, final_norm, side_casts=()):
    b, s, d = x.shape
    dff = w_out.shape[0]
    tm = min(s, ROWS_FFN)
    tf = 512
    nf = dff // tf
    nj = s // tm
    vec = pl.BlockSpec((None, 1, d), lambda i, j, f: (i, 0, 0))
    row = pl.BlockSpec((1, d), lambda i, j, f: (0, 0))
    tile = pl.BlockSpec((None, tm, d), lambda i, j, f: (i, j, 0))
    c_in, c_out, c_shape, c_args, c_vmem = _side_casts(side_casts, b * nj * nf, lambda i, j, f: (i * nj + j) * nf + f)
    return pl.pallas_call(
        functools.partial(_ffn_kernel, final_norm=final_norm, n_casts=len(side_casts)),
        grid=(b, nj, nf),
        in_specs=[
            tile,
            vec,
            vec,
            row,
            vec,
            pl.BlockSpec((d, tf), lambda i, j, f: (0, f)),
            pl.BlockSpec((d, tf), lambda i, j, f: (0, nf + f)),
            pl.BlockSpec((tf, d), lambda i, j, f: (f, 0)),
            row,
        ] + c_in,
        out_specs=[tile] + c_out,
        out_shape=[jax.ShapeDtypeStruct((b, s, d), F32)] + c_shape,
        scratch_shapes=[pltpu.VMEM((tm, d), BF16), pltpu.VMEM((1, d), F32)],
        compiler_params=_params(
            ("arbitrary", "arbitrary", "arbitrary"),
            _vmem_limit(2 * tm * d * 4 + 3 * d * tf * 2, tm * d * 2 + c_vmem, 4 * tm * 256 * 4),
        ),
        name="swiglu_ffn",
    )(x, scale, shift, gain.reshape(1, d), gate, w_in, w_in, w_out, final_gain.reshape(1, d), *c_args)


def kernel(x, c, positions, w_mod, b_mod, norm1_g, norm2_g, ret_w_in, ret_w_out, conv_w_in, conv_w, conv_w_out,
           gla_w_in, gla_w_gate_up, gla_b_gate, gla_w_out, pool_w, pool_scale, ffn_w_in, ffn_w_out, final_g):
    b, s, d = x.shape
    depth = w_mod.shape[0]
    assert depth >= 1 and d % (len(POOL_WINDOWS) * V7X_LANES) == 0 and s % GLA_CHUNK == 0

    mod = _modulation(c, w_mod, b_mod).reshape(depth, b, 6, 1, d)

    mixer_weights = {0: {"w_in": ret_w_in, "w_out": ret_w_out}, 1: {"w_in": conv_w_in, "w_out": conv_w_out},
                     2: {"w_in": gla_w_in, "w_out": gla_w_out}, 3: {}}
    ready = {}

    def pending(layer):
        return [((layer, name), stack, layer // N_MIXERS) for name, stack in mixer_weights[layer % N_MIXERS].items()]

    def hostable(todo):
        return [t for t in todo if t[1].shape[-1] % V7X_LANES == 0]

    def take(key, stack, index):
        return ready.pop(key) if key in ready else stack[index].astype(BF16)

    def host(call, todo):
        outs = call(side_casts=tuple((stack, index) for _, stack, index in todo))
        for (key, _, _), w in zip(todo, outs[len(outs) - len(todo):]):
            ready[key] = w
        return outs[:len(outs) - len(todo)]

    ret_dk = d // RET_HEADS
    cos, sin = host(functools.partial(_rope_tables, positions, ret_dk // 2), hostable(pending(0)))
    lane_pad = V7X_LANES - GLA_GATE_RANK
    for i in range(depth):
        sh1, sc1, g1, sh2, sc2, g2 = (mod[i, :, p] for p in range(6))
        m, j = i % N_MIXERS, i // N_MIXERS
        w_in, w_out = (take((i, name), stack, j) for (_, name), stack, j in pending(i)) if m < 3 else (None, None)
        todo = hostable(pending(i + 1)) if i + 1 < depth else []
        if i == 0:
            todo = todo + [((0, "ffn_in"), ffn_w_in, 0), ((0, "ffn_out"), ffn_w_out, 0)]
        if m == 0:
            assert w_in.shape[1] == 6 * COLS_PROJ
            col_steps = ((1, (("rope", 1.0),)), (1, (("rope", ret_dk ** -0.5),)),
                         (2, (("plain", 1.0),)), (2, (("silu", 1.0),)))
            (qkvg,) = host(functools.partial(_proj_in, x, sc1, sh1, norm1_g[i], w_in, col_steps, rope=(cos, sin)),
                           todo)
            x = _out_proj(_ret_attn(qkvg, RET_HEADS), w_out, x, g1)
        elif m == 1:
            (z,) = host(functools.partial(_conv_in, x, sc1, sh1, norm1_g[i], w_in, conv_w[j]), todo)
            x = _out_proj(z, w_out, x, g1)
        elif m == 2:
            gla_cols = w_in.shape[1] - GLA_GATE_RANK
            wz = jnp.pad(w_in[:, gla_cols:], ((0, 0), (0, lane_pad)))
            w_up = jnp.pad(gla_w_gate_up[j], ((0, lane_pad), (0, 0))).astype(BF16)
            assert gla_cols == 3 * COLS_PROJ
            col_steps = ((1, (("plain", (gla_cols // 6 // GLA_HEADS) ** -0.5), ("plain", 1.0))),
                         (1, (("plain", 1.0),)), (1, (("silu", 1.0),)))
            proj, z = host(functools.partial(_proj_in, x, sc1, sh1, norm1_g[i], w_in, col_steps, wz=wz), todo)
            x = _out_proj(_gla_attn(proj, z, w_up, gla_b_gate[j], GLA_HEADS), w_out, x, g1)
        else:
            x = _pool_mixer(x, sc1, sh1, norm1_g[i], pool_w[j].astype(BF16), pool_scale[j], g1)
            for key, stack, index in todo:
                ready[key] = stack[index].astype(BF16)
        ffn_in, ffn_out = take((i, "ffn_in"), ffn_w_in, i), take((i, "ffn_out"), ffn_w_out, i)
        todo = [((i + 1, "ffn_in"), ffn_w_in, i + 1), ((i + 1, "ffn_out"), ffn_w_out, i + 1)] if i + 1 < depth else []
        (x,) = host(functools.partial(_ffn, x, sc2, sh2, norm2_g[i], g2, ffn_in, ffn_out, final_g, i == depth - 1),
                    todo)
    return x
```

```python
import functools

import jax
import jax.numpy as jnp
from jax import lax
from jax.experimental import pallas as pl
from jax.experimental.pallas import tpu as pltpu

F32 = jnp.float32
BF16 = jnp.bfloat16

EPS = 1e-6
N_MIXERS = 4
RET_HEADS = 8
ROPE_BASE = 10000.0
CONV_WIDTH = 3
GLA_HEADS = 4
GLA_GATE_RANK = 16
GLA_GATE_TAU = 16.0
GLA_CHUNK = 64
POOL_WINDOWS = (2, 4, 8, 16)

V7X_VMEM_BYTES = 64 * 1024 * 1024
V7X_LANES = 128
V7X_SUBLANES = 8
BF16_ROWS = 2 * V7X_SUBLANES
MIB = 1024 * 1024

ROWS_ROPE = 1024
ROWS_PROJ = 1024
COLS_PROJ = 2048
ROWS_OUT_PROJ = 512
ROWS_RET = 1024
RET_CHUNK = 256
RET_HEADS_PER_STEP = 4
ROWS_GLA = 256
ROWS_CONV = 1024
ROWS_POOL = 512
ROWS_FFN = 1024
COLS_FFN = 512
PROLOGUE_ROWS = 256


def _vmem_limit(pipelined_block_bytes, scratch_bytes, temp_bytes):
    need = 2 * pipelined_block_bytes + scratch_bytes + temp_bytes + 2 * MIB
    return int(min(need, V7X_VMEM_BYTES - 4 * MIB))


def _params(semantics, vmem_bytes):
    return pltpu.CompilerParams(dimension_semantics=semantics, vmem_limit_bytes=vmem_bytes)


def _norm_mod(x, gain, scale, shift):
    ms = jnp.mean(x * x, axis=-1, keepdims=True)
    return (x * lax.rsqrt(ms + EPS)) * gain * (1.0 + scale) + shift


def _norm_mod_tile(x_ref, gs_ref, sh_ref, h_ref, rows):
    x = x_ref[rows, :]
    ms = jnp.mean(x * x, axis=-1, keepdims=True)
    h_ref[rows, :] = ((x * lax.rsqrt(ms + EPS)) * gs_ref[...] + sh_ref[...]).astype(h_ref.dtype)


def _cast_plan(rows, cols, steps, rows_only=False):
    options = []
    for size in range(BF16_ROWS, rows + 1, BF16_ROWS):
        if rows % size == 0 and rows // size <= steps:
            options.append((size * cols, 0, size, rows // size))
            break
    for size in range(V7X_LANES, 0 if rows_only else cols + 1, V7X_LANES):
        if cols % size == 0 and cols // size <= steps:
            options.append((rows * size, 1, size, cols // size))
            break
    assert options, "weight cannot be cast in slabs within this grid"
    _, axis, size, count = min(options)
    return axis, size, count


def _side_casts(side_casts, steps, step_of):
    in_specs, out_specs, out_shapes, args, vmem = [], [], [], [], 0
    for stack, layer, block in side_casts:
        _, rows, cols = stack.shape
        axis, size, count = _cast_plan(rows, cols, steps, rows_only=block is not None)

        def slab(*ids, count=count):
            return jnp.minimum(step_of(*ids), count - 1)

        if block is not None:
            n_blocks = cols // (2 * block)
            in_specs.append(pl.BlockSpec((None, size, cols), lambda *ids, la=layer, sl=slab: (la, sl(*ids), 0)))
            out_specs.append(pl.BlockSpec((n_blocks, size, 2 * block), lambda *ids, sl=slab: (0, sl(*ids), 0)))
            out_shapes.append(jax.ShapeDtypeStruct((n_blocks, rows, 2 * block), BF16))
            args.append(stack)
            vmem += 2 * size * cols * 6
            continue
        if axis == 0:
            in_specs.append(pl.BlockSpec((None, size, cols), lambda *ids, la=layer, sl=slab: (la, sl(*ids), 0)))
            out_specs.append(pl.BlockSpec((size, cols), lambda *ids, sl=slab: (sl(*ids), 0)))
            vmem += 2 * size * cols * 6
        else:
            in_specs.append(pl.BlockSpec((None, rows, size), lambda *ids, la=layer, sl=slab: (la, 0, sl(*ids))))
            out_specs.append(pl.BlockSpec((rows, size), lambda *ids, sl=slab: (0, sl(*ids))))
            vmem += 2 * rows * size * 6
        out_shapes.append(jax.ShapeDtypeStruct((rows, cols), BF16))
        args.append(stack)
    return in_specs, out_specs, out_shapes, args, vmem


def _cast_slabs(src_refs, dst_refs):
    for src, dst in zip(src_refs, dst_refs):
        if len(dst.shape) == 3:
            n_blocks, _, width = dst.shape
            t, half = width // 2, src.shape[1] // 2
            for f in range(n_blocks):
                dst[f, :, :t] = src[:, f * t:(f + 1) * t].astype(BF16)
                dst[f, :, t:] = src[:, half + f * t:half + (f + 1) * t].astype(BF16)
        else:
            dst[...] = src[...].astype(BF16)


def _block_halves(w, t):
    rows, cols = w.shape
    return w.reshape(rows, 2, cols // (2 * t), t).transpose(2, 0, 1, 3).reshape(cols // (2 * t), rows, 2 * t)


def _silu(x):
    return x * jax.nn.sigmoid(x)


def _head_rms(o):
    return o * lax.rsqrt(jnp.mean(o * o, axis=-1, keepdims=True) + EPS)


def _mod_kernel(c_ref, w_ref, b_ref, o_ref):
    c_act = _silu(c_ref[...]).astype(BF16)
    o_ref[...] = jnp.dot(c_act, w_ref[...].astype(BF16), preferred_element_type=F32) + b_ref[...]


def _modulation(c, w_mod, b_mod):
    depth, d, n = w_mod.shape
    b = c.shape[0]
    tn = 1024
    return pl.pallas_call(
        _mod_kernel,
        grid=(depth, n // tn),
        in_specs=[
            pl.BlockSpec((b, d), lambda l, j: (0, 0)),
            pl.BlockSpec((None, d, tn), lambda l, j: (l, 0, j)),
            pl.BlockSpec((None, 1, tn), lambda l, j: (l, 0, j)),
        ],
        out_specs=pl.BlockSpec((None, b, tn), lambda l, j: (l, 0, j)),
        out_shape=jax.ShapeDtypeStruct((depth, b, n), F32),
        compiler_params=_params(("parallel", "parallel"), _vmem_limit(d * tn * 4, 0, d * tn * 2)),
        name="adaln_mod",
    )(c, w_mod, b_mod.reshape(depth, 1, n))


def _rope_kernel(pos_ref, invf_ref, *rest, n_casts):
    cast_srcs, (cos_ref, sin_ref), cast_dsts = rest[:n_casts], rest[n_casts:n_casts + 2], rest[n_casts + 2:]
    ang = pos_ref[...].astype(F32) * invf_ref[...]
    cos_ref[...] = jnp.cos(ang)
    sin_ref[...] = jnp.sin(ang)
    _cast_slabs(cast_srcs, cast_dsts)


def _rope_tables(positions, half, side_casts=()):
    b, s = positions.shape
    inv_freq = jnp.power(ROPE_BASE, -jnp.linspace(0.0, 1.0, half, dtype=F32)).reshape(1, half)
    ts = min(s, ROWS_ROPE)
    nj = s // ts
    spec = pl.BlockSpec((None, ts, half), lambda i, j: (i, j, 0))
    c_in, c_out, c_shape, c_args, c_vmem = _side_casts(side_casts, b * nj, lambda i, j: i * nj + j)
    return pl.pallas_call(
        functools.partial(_rope_kernel, n_casts=len(side_casts)),
        grid=(b, nj),
        in_specs=[pl.BlockSpec((None, ts, 1), lambda i, j: (i, j, 0)),
                  pl.BlockSpec((1, half), lambda i, j: (0, 0))] + c_in,
        out_specs=[spec, spec] + c_out,
        out_shape=[jax.ShapeDtypeStruct((b, s, half), F32)] * 2 + c_shape,
        compiler_params=_params(("arbitrary", "arbitrary"), _vmem_limit(3 * ts * V7X_LANES * 4, c_vmem, 8 * MIB)),
        name="rope_tables",
    )(positions.reshape(b, s, 1), inv_freq, *c_args)


def _rope_apply(t, cos, sin):
    half = cos.shape[-1]
    parts = []
    for c in range(0, t.shape[-1], 2 * half):
        t1, t2 = t[:, c:c + half], t[:, c + half:c + 2 * half]
        parts += [t1 * cos - t2 * sin, t2 * cos + t1 * sin]
    return jnp.concatenate(parts, axis=-1)


def _proj_in_kernel(*refs, col_steps, with_rope, with_z, n_casts):
    x_ref, sc_ref, sh_ref, g_ref, w_ref = refs[:5]
    rest = list(refs[5:])
    cos_ref, sin_ref = (rest.pop(0), rest.pop(0)) if with_rope else (None, None)
    wz_ref = rest.pop(0) if with_z else None
    cast_srcs = [rest.pop(0) for _ in range(n_casts)]
    o_ref = rest.pop(0)
    z_ref = rest.pop(0) if with_z else None
    cast_dsts = [rest.pop(0) for _ in range(n_casts)]
    h_ref, gs_ref = rest
    k = pl.program_id(2)

    def emit(posts, rows=slice(None)):
        y = jnp.dot(h_ref[rows, :], w_ref[...], preferred_element_type=F32)
        width = y.shape[1] // len(posts)
        for p, (kind, scale) in enumerate(posts):
            cols = slice(p * width, (p + 1) * width)
            part = y[:, cols]
            if kind == "rope":
                part = _rope_apply(part, cos_ref[rows, :], sin_ref[rows, :])
            elif kind == "silu":
                part = _silu(part)
            if scale != 1.0:
                part = part * scale
            o_ref[rows, cols] = part.astype(o_ref.dtype)

    def emit_step(posts):
        emit(posts)
        _cast_slabs(cast_srcs, cast_dsts)

    @pl.when(k == 0)
    def _():
        gs_ref[...] = g_ref[...] * (1.0 + sc_ref[...])
        tm = x_ref.shape[0]
        group = min(PROLOGUE_ROWS, tm)
        for r in range(0, tm, group):
            for t in range(r, r + group, BF16_ROWS):
                _norm_mod_tile(x_ref, gs_ref, sh_ref, h_ref, slice(t, t + BF16_ROWS))
            emit(col_steps[0][1], slice(r, r + group))
        if with_z:
            z_ref[...] = jnp.dot(h_ref[...], wz_ref[...], preferred_element_type=F32).astype(z_ref.dtype)
        _cast_slabs(cast_srcs, cast_dsts)

    start = 0
    for n_steps, posts in col_steps:
        pl.when(jnp.logical_and(k >= max(start, 1), k < start + n_steps))(functools.partial(emit_step, posts))
        start += n_steps


def _proj_in(x, scale, shift, gain, w, col_steps, rope=None, wz=None, side_casts=()):
    b, s, d = x.shape
    tm = min(s, ROWS_PROJ)
    tn = COLS_PROJ
    nj = s // tm
    nk = sum(n for n, _ in col_steps)
    n_cols = tn * nk
    with_z = wz is not None
    with_rope = rope is not None
    vec = pl.BlockSpec((None, 1, d), lambda i, j, k: (i, 0, 0))
    in_specs = [
        pl.BlockSpec((None, tm, d), lambda i, j, k: (i, j, 0)),
        vec,
        vec,
        pl.BlockSpec((1, d), lambda i, j, k: (0, 0)),
        pl.BlockSpec((d, tn), lambda i, j, k: (0, k)),
    ]
    out_specs = [pl.BlockSpec((None, tm, tn), lambda i, j, k: (i, j, k))]
    out_shape = [jax.ShapeDtypeStruct((b, s, n_cols), BF16)]
    args = [x, scale, shift, gain.reshape(1, d), w]
    if with_rope:
        half = rope[0].shape[-1]
        in_specs += [pl.BlockSpec((None, tm, half), lambda i, j, k: (i, j, 0))] * 2
        args += list(rope)
    if with_z:
        in_specs.append(pl.BlockSpec((d, V7X_LANES), lambda i, j, k: (0, 0)))
        out_specs.append(pl.BlockSpec((None, tm, V7X_LANES), lambda i, j, k: (i, j, 0)))
        out_shape.append(jax.ShapeDtypeStruct((b, s, V7X_LANES), BF16))
        args.append(wz)
    c_in, c_out, c_shape, c_args, c_vmem = _side_casts(side_casts, b * nj * nk, lambda i, j, k: (i * nj + j) * nk + k)
    outs = pl.pallas_call(
        functools.partial(_proj_in_kernel, col_steps=col_steps, with_rope=with_rope, with_z=with_z,
                          n_casts=len(side_casts)),
        grid=(b, nj, nk),
        in_specs=in_specs + c_in,
        out_specs=out_specs + c_out,
        out_shape=out_shape + c_shape,
        scratch_shapes=[pltpu.VMEM((tm, d), BF16), pltpu.VMEM((1, d), F32)],
        compiler_params=_params(
            ("arbitrary", "arbitrary", "arbitrary"),
            _vmem_limit(tm * d * 4 + d * tn * 2 + tm * tn * 2 + (d + 5 * tm) * V7X_LANES * 2, tm * d * 2 + c_vmem,
                        2 * tm * tn * 4),
        ),
        name="proj_in",
    )(*args, *c_args)
    return list(outs)


def _out_proj_kernel(a_ref, w_ref, x_ref, gate_ref, o_ref):
    y = jnp.dot(a_ref[...], w_ref[...], preferred_element_type=F32)
    o_ref[...] = x_ref[...] + gate_ref[...] * y


def _out_proj(a, w, x, gate):
    b, s, k = a.shape
    d = w.shape[1]
    tm = min(s, ROWS_OUT_PROJ)
    tile = pl.BlockSpec((None, tm, d), lambda i, j: (i, j, 0))
    return pl.pallas_call(
        _out_proj_kernel,
        grid=(b, s // tm),
        in_specs=[
            pl.BlockSpec((None, tm, k), lambda i, j: (i, j, 0)),
            pl.BlockSpec((k, d), lambda i, j: (0, 0), pipeline_mode=pl.Buffered(1)),
            tile,
            pl.BlockSpec((None, 1, d), lambda i, j: (i, 0, 0)),
        ],
        out_specs=tile,
        out_shape=jax.ShapeDtypeStruct((b, s, d), F32),
        compiler_params=_params(
            ("parallel", "parallel"),
            _vmem_limit(tm * k * 2 + 2 * tm * d * 4, k * d * 2, tm * d * 4),
        ),
        name="out_proj",
    )(a, w, x, gate)


def _ret_attn_kernel(lg_ref, q_ref, k_ref, v_ref, g_ref, o_ref, state_ref, *, chunk):
    @pl.when(pl.program_id(2) == 0)
    def _():
        state_ref[...] = jnp.zeros_like(state_ref)

    heads, dk, dv = state_ref.shape
    tile = q_ref.shape[0]
    ii = lax.broadcasted_iota(jnp.int32, (chunk, chunk), 0)
    jj = lax.broadcasted_iota(jnp.int32, (chunk, chunk), 1)
    r = lax.broadcasted_iota(jnp.int32, (chunk, 1), 0).astype(F32)
    sls = [pl.ds(c * chunk, chunk) for c in range(tile // chunk)]

    for h in range(heads):
        kc = slice(h * dk, (h + 1) * dk)
        vc = slice(h * dv, (h + 1) * dv)
        lg = lg_ref[h][:, 0:1]
        decay = jnp.where(ii >= jj, jnp.exp((ii - jj).astype(F32) * lg), 0.0)
        row_dec = jnp.exp((r + 1.0) * lg)
        col_dec = jnp.exp((chunk - 1.0 - r) * lg)
        chunk_dec = jnp.exp(chunk * lg)
        probs = []
        for sl in sls:
            scores = lax.dot_general(q_ref[sl, kc], k_ref[sl, kc], (((1,), (1,)), ((), ())),
                                     preferred_element_type=F32)
            probs.append((scores * decay).astype(BF16))
        outs = [jnp.dot(p, v_ref[sl, vc], preferred_element_type=F32) for p, sl in zip(probs, sls)]
        updates = [lax.dot_general((k_ref[sl, kc].astype(F32) * col_dec).astype(BF16), v_ref[sl, vc],
                                   (((0,), (0,)), ((), ())), preferred_element_type=F32) for sl in sls]
        state = state_ref[h]
        for c, sl in enumerate(sls):
            q_dec = (q_ref[sl, kc].astype(F32) * row_dec).astype(BF16)
            outs[c] = outs[c] + jnp.dot(q_dec, state.astype(BF16), preferred_element_type=F32)
            state = chunk_dec * state + updates[c]
        state_ref[h] = state
        for c, sl in enumerate(sls):
            o_ref[sl, vc] = (g_ref[sl, vc].astype(F32) * _head_rms(outs[c])).astype(o_ref.dtype)


def _ret_attn(qkvg, heads):
    b, s, n = qkvg.shape
    dk = n // (6 * heads)
    dv = 2 * dk
    tile = min(s, ROWS_RET)
    chunk = min(tile, RET_CHUNK)
    hs = RET_HEADS_PER_STEP
    groups = heads // hs
    log_gamma = jnp.log1p(-jnp.exp2(-5.0 - jnp.arange(heads, dtype=F32)))
    lg = jnp.broadcast_to(log_gamma[:, None, None], (heads, 1, V7X_LANES))
    return pl.pallas_call(
        functools.partial(_ret_attn_kernel, chunk=chunk),
        grid=(b, groups, s // tile),
        in_specs=[
            pl.BlockSpec((hs, 1, V7X_LANES), lambda i, h, j: (h, 0, 0)),
            pl.BlockSpec((None, tile, hs * dk), lambda i, h, j: (i, j, h)),
            pl.BlockSpec((None, tile, hs * dk), lambda i, h, j: (i, j, groups + h)),
            pl.BlockSpec((None, tile, hs * dv), lambda i, h, j: (i, j, groups + h)),
            pl.BlockSpec((None, tile, hs * dv), lambda i, h, j: (i, j, 2 * groups + h)),
        ],
        out_specs=pl.BlockSpec((None, tile, hs * dv), lambda i, h, j: (i, j, h)),
        out_shape=jax.ShapeDtypeStruct((b, s, heads * dv), BF16),
        scratch_shapes=[pltpu.VMEM((hs, dk, dv), F32)],
        compiler_params=_params(
            ("parallel", "parallel", "arbitrary"),
            _vmem_limit(tile * hs * (2 * dk + 3 * dv) * 2, hs * dk * dv * 4, 16 * MIB),
        ),
        name="retention_core",
    )(lg, qkvg, qkvg, qkvg, qkvg)


def _gla_attn_kernel(q_ref, k_ref, v_ref, g_ref, z_ref, wg_ref, bg_ref, o_ref, state_ref, *, chunk):
    @pl.when(pl.program_id(1) == 0)
    def _():
        state_ref[...] = jnp.zeros_like(state_ref)

    heads, dv, dk = state_ref.shape
    tile = q_ref.shape[0]
    shift = chunk.bit_length() - 1
    pre = jnp.dot(z_ref[...], wg_ref[...], preferred_element_type=F32) + bg_ref[...]
    log_a = (jnp.minimum(pre, 0.0) - jnp.log(1.0 + jnp.exp(-jnp.abs(pre)))) * (1.0 / GLA_GATE_TAU)
    ii = lax.broadcasted_iota(jnp.int32, (tile, tile), 0)
    jj = lax.broadcasted_iota(jnp.int32, (tile, tile), 1)
    same_chunk = (ii >> shift) == (jj >> shift)
    causal = jnp.logical_and(same_chunk, ii >= jj)
    sums = jnp.where(causal, 1.0, 0.0).astype(BF16)
    hi = log_a.astype(BF16)
    rest = log_a - hi.astype(F32)
    mid = rest.astype(BF16)
    lo = (rest - mid.astype(F32)).astype(BF16)
    b = (jnp.dot(sums, hi, preferred_element_type=F32) + jnp.dot(sums, mid, preferred_element_type=F32)
         + jnp.dot(sums, lo, preferred_element_type=F32))
    n_chunks = tile // chunk
    b_ends = [b[(c + 1) * chunk - 1:(c + 1) * chunk, :] for c in range(n_chunks)]
    b_last = jnp.concatenate([jnp.broadcast_to(e, (chunk, e.shape[1])) for e in b_ends], axis=0)

    k = k_ref[...].astype(F32)
    q_dec = (q_ref[...].astype(F32) * jnp.exp(b)).astype(BF16)
    k_dec = (k * jnp.exp(-b)).astype(BF16)
    k_carry = (k * jnp.exp(b_last - b)).astype(BF16)
    chunk_dec = [jnp.exp(e) for e in b_ends]

    kcs = [slice(h * dk, (h + 1) * dk) for h in range(heads)]
    vcs = [slice(h * dv, (h + 1) * dv) for h in range(heads)]
    probs = []
    for kc in kcs:
        scores = lax.dot_general(q_dec[:, kc], k_dec[:, kc], (((1,), (1,)), ((), ())), preferred_element_type=F32)
        probs.append(jnp.where(causal, scores, 0.0).astype(BF16))
    outs = [jnp.dot(p, v_ref[:, vc], preferred_element_type=F32) for p, vc in zip(probs, vcs)]
    states = [state_ref[h] for h in range(heads)]
    inter = [[] for _ in range(heads)]
    for c in range(tile // chunk):
        rows = slice(c * chunk, (c + 1) * chunk)
        for h, (kc, vc) in enumerate(zip(kcs, vcs)):
            inter[h].append(lax.dot_general(q_dec[rows, kc], states[h].astype(BF16), (((1,), (1,)), ((), ())),
                                            preferred_element_type=F32))
            update = lax.dot_general(v_ref[rows, vc], k_carry[rows, kc], (((0,), (0,)), ((), ())),
                                     preferred_element_type=F32)
            states[h] = chunk_dec[c][:, kc] * states[h] + update
    for h, vc in enumerate(vcs):
        state_ref[h] = states[h]
        o = outs[h] + jnp.concatenate(inter[h], axis=0)
        o_ref[:, vc] = (g_ref[:, vc].astype(F32) * _head_rms(o)).astype(o_ref.dtype)


def _gla_attn(proj, z, w_gate_up, b_gate, heads):
    b, s, n = proj.shape
    dk = n // (6 * heads)
    dv = 2 * dk
    tile = min(s, ROWS_GLA)
    return pl.pallas_call(
        functools.partial(_gla_attn_kernel, chunk=GLA_CHUNK),
        grid=(b, s // tile),
        in_specs=[
            pl.BlockSpec((None, tile, heads * dk), lambda i, j: (i, j, 0)),
            pl.BlockSpec((None, tile, heads * dk), lambda i, j: (i, j, 1)),
            pl.BlockSpec((None, tile, heads * dv), lambda i, j: (i, j, 1)),
            pl.BlockSpec((None, tile, heads * dv), lambda i, j: (i, j, 2)),
            pl.BlockSpec((None, tile, V7X_LANES), lambda i, j: (i, j, 0)),
            pl.BlockSpec((V7X_LANES, heads * dk), lambda i, j: (0, 0)),
            pl.BlockSpec((1, heads * dk), lambda i, j: (0, 0)),
        ],
        out_specs=pl.BlockSpec((None, tile, heads * dv), lambda i, j: (i, j, 0)),
        out_shape=jax.ShapeDtypeStruct((b, s, heads * dv), BF16),
        scratch_shapes=[pltpu.VMEM((heads, dv, dk), F32)],
        compiler_params=_params(
            ("parallel", "arbitrary"),
            _vmem_limit(tile * heads * (2 * dk + 3 * dv) * 2 + 2 * V7X_LANES * heads * dk * 2,
                        heads * dk * dv * 4, 24 * MIB),
        ),
        name="gla_core",
    )(proj, proj, proj, proj, z, w_gate_up, b_gate.reshape(1, -1))


def _conv_in_kernel(x_ref, sc_ref, sh_ref, g_ref, wb_ref, wc_ref, wu_ref, cw_ref, *rest, n_casts):
    cast_srcs, o_ref, cast_dsts = rest[:n_casts], rest[n_casts], rest[n_casts + 1:2 * n_casts + 1]
    h_ref, gs_ref, carry_ref = rest[2 * n_casts + 1:]
    j = pl.program_id(1)
    n = pl.program_id(2)
    tm, tn = o_ref.shape

    prev_tile = jnp.where(j > 0, carry_ref[n], 0.0)

    def conv_rows(r0, nrows, prev_rows):
        rows = slice(r0, r0 + nrows)
        h = h_ref[rows, :]
        row = lax.broadcasted_iota(jnp.int32, (nrows, 1), 0)
        last = {}
        for c in range(0, tn, 256):
            cols = slice(c, c + 256)
            b_gate = jnp.dot(h, wb_ref[:, cols], preferred_element_type=F32)
            c_gate = jnp.dot(h, wc_ref[:, cols], preferred_element_type=F32)
            u = c_gate * jnp.dot(h, wu_ref[:, cols], preferred_element_type=F32)
            prev = prev_rows[c]
            last[c] = u[nrows - V7X_SUBLANES:, :]
            u1 = jnp.where(row == 0, prev[7:8, :], pltpu.roll(u, 1, axis=0))
            u2 = jnp.where(row == 0, prev[6:7, :],
                           jnp.where(row == 1, prev[7:8, :], pltpu.roll(u, 2, axis=0)))
            y = cw_ref[0:1, cols] * u2 + cw_ref[1:2, cols] * u1 + cw_ref[2:3, cols] * u
            o_ref[rows, cols] = (b_gate * y).astype(o_ref.dtype)
        return last

    def store_carry(last):
        for c, rows8 in last.items():
            carry_ref[n, :, c:c + 256] = rows8

    tile_prev = {c: prev_tile[:, c:c + 256] for c in range(0, tn, 256)}

    @pl.when(n == 0)
    def _():
        gs_ref[...] = g_ref[...] * (1.0 + sc_ref[...])
        prev_rows = tile_prev
        group = min(PROLOGUE_ROWS, tm)
        for r in range(0, tm, group):
            for t in range(r, r + group, BF16_ROWS):
                _norm_mod_tile(x_ref, gs_ref, sh_ref, h_ref, slice(t, t + BF16_ROWS))
            prev_rows = conv_rows(r, group, prev_rows)
        store_carry(prev_rows)
        _cast_slabs(cast_srcs, cast_dsts)

    @pl.when(n > 0)
    def _():
        store_carry(conv_rows(0, tm, tile_prev))
        _cast_slabs(cast_srcs, cast_dsts)


def _conv_in(x, scale, shift, gain, w, conv_w, side_casts=()):
    b, s, d = x.shape
    tm = min(s, ROWS_CONV)
    tn = 1024
    nb = d // tn
    nj = s // tm
    vec = pl.BlockSpec((None, 1, d), lambda i, j, n: (i, 0, 0))

    def w_spec(part):
        return pl.BlockSpec((d, tn), lambda i, j, n: (0, part * nb + n))

    c_in, c_out, c_shape, c_args, c_vmem = _side_casts(side_casts, b * nj * nb, lambda i, j, n: (i * nj + j) * nb + n)
    return pl.pallas_call(
        functools.partial(_conv_in_kernel, n_casts=len(side_casts)),
        grid=(b, nj, nb),
        in_specs=[
            pl.BlockSpec((None, tm, d), lambda i, j, n: (i, j, 0)),
            vec,
            vec,
            pl.BlockSpec((1, d), lambda i, j, n: (0, 0)),
            w_spec(0),
            w_spec(1),
            w_spec(2),
            pl.BlockSpec((CONV_WIDTH, tn), lambda i, j, n: (0, n)),
        ] + c_in,
        out_specs=[pl.BlockSpec((None, tm, tn), lambda i, j, n: (i, j, n))] + c_out,
        out_shape=[jax.ShapeDtypeStruct((b, s, d), BF16)] + c_shape,
        scratch_shapes=[pltpu.VMEM((tm, d), BF16), pltpu.VMEM((1, d), F32),
                        pltpu.VMEM((nb, V7X_SUBLANES, tn), F32)],
        compiler_params=_params(
            ("arbitrary", "arbitrary", "arbitrary"),
            _vmem_limit(tm * d * 4 + 3 * d * tn * 2 + tm * tn * 2, tm * d * 2 + c_vmem, 8 * tm * tn * 4),
        ),
        name="conv_in",
    )(x, scale, shift, gain.reshape(1, d), w, w, w, conv_w, *c_args)


def _pool_kernel(x_ref, sc_ref, sh_ref, g_ref, w_ref, ps_ref, gate_ref, o_ref, carry_ref):
    j = pl.program_id(1)
    tm, d = x_ref.shape
    halo = carry_ref.shape[0]
    group = d // len(POOL_WINDOWS)
    x = x_ref[...]
    h = _norm_mod(x, g_ref[...], sc_ref[...], sh_ref[...])
    prev = jnp.where(j > 0, carry_ref[...], 0.0)
    carry_ref[...] = h[tm - halo:, :]
    t = j * tm + lax.broadcasted_iota(jnp.int32, (tm, 1), 0)

    for gi, win in enumerate(POOL_WINDOWS):
        cols = slice(gi * group, (gi + 1) * group)
        hg = h[:, cols]
        ext = jnp.concatenate([prev[:, cols], hg], axis=0)
        k = 1
        while k < win:
            ext = ext + pltpu.roll(ext, k, axis=0)
            k *= 2
        inv_count = 1.0 / jnp.minimum(t + 1, win).astype(F32)
        mixed = ext[halo:, :] * inv_count - hg
        y = jnp.dot(mixed.astype(BF16), w_ref[gi], preferred_element_type=F32)
        o_ref[:, cols] = x[:, cols] + gate_ref[:, cols] * (y * ps_ref[:, cols])


def _pool_mixer(x, scale, shift, gain, pool_w, pool_scale, gate):
    b, s, d = x.shape
    groups, p, _ = pool_w.shape
    tm = min(s, ROWS_POOL)
    halo = 2 * V7X_SUBLANES
    vec = pl.BlockSpec((None, 1, d), lambda i, j: (i, 0, 0))
    row = pl.BlockSpec((1, d), lambda i, j: (0, 0))
    tile = pl.BlockSpec((None, tm, d), lambda i, j: (i, j, 0))
    return pl.pallas_call(
        _pool_kernel,
        grid=(b, s // tm),
        in_specs=[tile, vec, vec, row, pl.BlockSpec((groups, p, p), lambda i, j: (0, 0, 0)), row, vec],
        out_specs=tile,
        out_shape=jax.ShapeDtypeStruct((b, s, d), F32),
        scratch_shapes=[pltpu.VMEM((halo, d), F32)],
        compiler_params=_params(
            ("parallel", "arbitrary"),
            _vmem_limit(2 * tm * d * 4 + groups * p * p * 2, halo * d * 4, 6 * tm * d * 4),
        ),
        name="pool_mixer",
    )(x, scale, shift, gain.reshape(1, d), pool_w, pool_scale.reshape(1, d), gate)


def _ffn_kernel(x_ref, sc_ref, sh_ref, g_ref, gate_ref, wab_ref, wo_ref, fg_ref, *rest, final_norm, n_casts):
    cast_srcs, o_ref, cast_dsts = rest[:n_casts], rest[n_casts], rest[n_casts + 1:2 * n_casts + 1]
    h_ref, gs_ref = rest[2 * n_casts + 1:]
    f = pl.program_id(2)

    def ffn_part(rows):
        h = h_ref[rows, :]
        tf = wab_ref.shape[1] // 2
        acts = []
        for c in range(0, tf, 256):
            a = jnp.dot(h, wab_ref[:, c:c + 256], preferred_element_type=F32)
            b = jnp.dot(h, wab_ref[:, tf + c:tf + c + 256], preferred_element_type=F32)
            acts.append((_silu(a) * b).astype(BF16))
        act = jnp.concatenate(acts, axis=-1)
        return gate_ref[...] * jnp.dot(act, wo_ref[...], preferred_element_type=F32)

    @pl.when(f == 0)
    def _():
        gs_ref[...] = g_ref[...] * (1.0 + sc_ref[...])
        group = min(PROLOGUE_ROWS, x_ref.shape[0])
        for r in range(0, x_ref.shape[0], group):
            rows = slice(r, r + group)
            for t in range(r, r + group, BF16_ROWS):
                _norm_mod_tile(x_ref, gs_ref, sh_ref, h_ref, slice(t, t + BF16_ROWS))
            o_ref[rows, :] = x_ref[rows, :] + ffn_part(rows)
        _cast_slabs(cast_srcs, cast_dsts)

    @pl.when(f > 0)
    def _():
        o_ref[...] += ffn_part(slice(None))
        _cast_slabs(cast_srcs, cast_dsts)

    if final_norm:
        @pl.when(f == pl.num_programs(2) - 1)
        def _():
            for r in range(0, o_ref.shape[0], 4 * V7X_SUBLANES):
                rows = slice(r, r + 4 * V7X_SUBLANES)
                o_ref[rows, :] = _head_rms(o_ref[rows, :]) * fg_ref[...]


def _ffn(x, scale, shift, gain, gate, w_in, w_out, final_gain, final_norm, side_casts=()):
    b, s, d = x.shape
    dff = w_out.shape[0]
    tm = min(s, ROWS_FFN)
    tf = COLS_FFN
    nf = dff // tf
    nj = s // tm
    vec = pl.BlockSpec((None, 1, d), lambda i, j, f: (i, 0, 0))
    row = pl.BlockSpec((1, d), lambda i, j, f: (0, 0))
    tile = pl.BlockSpec((None, tm, d), lambda i, j, f: (i, j, 0))
    c_in, c_out, c_shape, c_args, c_vmem = _side_casts(side_casts, b * nj * nf, lambda i, j, f: (i * nj + j) * nf + f)
    return pl.pallas_call(
        functools.partial(_ffn_kernel, final_norm=final_norm, n_casts=len(side_casts)),
        grid=(b, nj, nf),
        in_specs=[
            tile,
            vec,
            vec,
            row,
            vec,
            pl.BlockSpec((None, d, 2 * tf), lambda i, j, f: (f, 0, 0)),
            pl.BlockSpec((tf, d), lambda i, j, f: (f, 0)),
            row,
        ] + c_in,
        out_specs=[tile] + c_out,
        out_shape=[jax.ShapeDtypeStruct((b, s, d), F32)] + c_shape,
        scratch_shapes=[pltpu.VMEM((tm, d), BF16), pltpu.VMEM((1, d), F32)],
        compiler_params=_params(
            ("arbitrary", "arbitrary", "arbitrary"),
            _vmem_limit(2 * tm * d * 4 + 3 * d * tf * 2, tm * d * 2 + c_vmem, 4 * tm * 256 * 4),
        ),
        name="swiglu_ffn",
    )(x, scale, shift, gain.reshape(1, d), gate, w_in, w_out, final_gain.reshape(1, d), *c_args)


def kernel(x, c, positions, w_mod, b_mod, norm1_g, norm2_g, ret_w_in, ret_w_out, conv_w_in, conv_w, conv_w_out,
           gla_w_in, gla_w_gate_up, gla_b_gate, gla_w_out, pool_w, pool_scale, ffn_w_in, ffn_w_out, final_g):
    b, s, d = x.shape
    depth = w_mod.shape[0]
    assert depth >= 1 and d % (len(POOL_WINDOWS) * V7X_LANES) == 0 and s % GLA_CHUNK == 0

    mod = _modulation(c, w_mod, b_mod).reshape(depth, b, 6, 1, d)

    mixer_weights = {0: {"w_in": ret_w_in, "w_out": ret_w_out}, 1: {"w_in": conv_w_in, "w_out": conv_w_out},
                     2: {"w_in": gla_w_in, "w_out": gla_w_out}, 3: {}}
    ready = {}

    def pending(layer):
        return [((layer, name), stack, layer // N_MIXERS, None)
                for name, stack in mixer_weights[layer % N_MIXERS].items()]

    def ffn_pending(layer):
        return [((layer, "ffn_in"), ffn_w_in, layer, COLS_FFN), ((layer, "ffn_out"), ffn_w_out, layer, None)]

    def hostable(todo):
        return [t for t in todo if t[1].shape[-1] % V7X_LANES == 0]

    def take(key, stack, index, block):
        if key in ready:
            return ready.pop(key)
        w = stack[index].astype(BF16)
        return w if block is None else _block_halves(w, block)

    def host(call, todo):
        outs = call(side_casts=tuple(t[1:] for t in todo))
        for t, w in zip(todo, outs[len(outs) - len(todo):]):
            ready[t[0]] = w
        return outs[:len(outs) - len(todo)]

    ret_dk = d // RET_HEADS
    cos, sin = host(functools.partial(_rope_tables, positions, ret_dk // 2), hostable(pending(0)))
    lane_pad = V7X_LANES - GLA_GATE_RANK
    for i in range(depth):
        sh1, sc1, g1, sh2, sc2, g2 = (mod[i, :, p] for p in range(6))
        m, j = i % N_MIXERS, i // N_MIXERS
        w_in, w_out = (take(*t) for t in pending(i)) if m < 3 else (None, None)
        todo = hostable(pending(i + 1)) if i + 1 < depth else []
        if i == 0:
            todo = todo + ffn_pending(0)
        if m == 0:
            assert w_in.shape[1] == 6 * COLS_PROJ
            col_steps = ((1, (("rope", 1.0),)), (1, (("rope", ret_dk ** -0.5),)),
                         (2, (("plain", 1.0),)), (2, (("silu", 1.0),)))
            (qkvg,) = host(functools.partial(_proj_in, x, sc1, sh1, norm1_g[i], w_in, col_steps, rope=(cos, sin)),
                           todo)
            x = _out_proj(_ret_attn(qkvg, RET_HEADS), w_out, x, g1)
        elif m == 1:
            (z,) = host(functools.partial(_conv_in, x, sc1, sh1, norm1_g[i], w_in, conv_w[j]), todo)
            x = _out_proj(z, w_out, x, g1)
        elif m == 2:
            gla_cols = w_in.shape[1] - GLA_GATE_RANK
            wz = jnp.pad(w_in[:, gla_cols:], ((0, 0), (0, lane_pad)))
            w_up = jnp.pad(gla_w_gate_up[j], ((0, lane_pad), (0, 0))).astype(BF16)
            assert gla_cols == 3 * COLS_PROJ
            col_steps = ((1, (("plain", (gla_cols // 6 // GLA_HEADS) ** -0.5), ("plain", 1.0))),
                         (1, (("plain", 1.0),)), (1, (("silu", 1.0),)))
            proj, z = host(functools.partial(_proj_in, x, sc1, sh1, norm1_g[i], w_in, col_steps, wz=wz), todo)
            x = _out_proj(_gla_attn(proj, z, w_up, gla_b_gate[j], GLA_HEADS), w_out, x, g1)
        else:
            x = _pool_mixer(x, sc1, sh1, norm1_g[i], pool_w[j].astype(BF16), pool_scale[j], g1)
            for t in todo:
                ready[t[0]] = take(*t)
        ffn_in, ffn_out = (take(*t) for t in ffn_pending(i))
        todo = ffn_pending(i + 1) if i + 1 < depth else []
        (x,) = host(functools.partial(_ffn, x, sc2, sh2, norm2_g[i], g2, ffn_in, ffn_out, final_g, i == depth - 1),
                    todo)
    return x
```

```python
import functools

import jax
import jax.numpy as jnp
from jax import lax
from jax.experimental import pallas as pl
from jax.experimental.pallas import tpu as pltpu

F32 = jnp.float32
BF16 = jnp.bfloat16

EPS = 1e-6
N_MIXERS = 4
RET_HEADS = 8
ROPE_BASE = 10000.0
CONV_WIDTH = 3
GLA_HEADS = 4
GLA_GATE_RANK = 16
GLA_GATE_TAU = 16.0
GLA_CHUNK = 64
POOL_WINDOWS = (2, 4, 8, 16)

V7X_VMEM_BYTES = 64 * 1024 * 1024
V7X_LANES = 128
V7X_SUBLANES = 8
BF16_ROWS = 2 * V7X_SUBLANES
MIB = 1024 * 1024

ROWS_ROPE = 1024
ROWS_PROJ = 1024
COLS_PROJ = 2048
ROWS_OUT_PROJ = 512
ROWS_RET = 1024
RET_CHUNK = 256
RET_HEADS_PER_STEP = 4
ROWS_GLA = 256
ROWS_CONV = 1024
ROWS_POOL = 512
ROWS_FFN = 1024
PROLOGUE_ROWS = 256


def _vmem_limit(pipelined_block_bytes, scratch_bytes, temp_bytes):
    need = 2 * pipelined_block_bytes + scratch_bytes + temp_bytes + 2 * MIB
    return int(min(need, V7X_VMEM_BYTES - 6 * MIB))


def _params(semantics, vmem_bytes):
    return pltpu.CompilerParams(dimension_semantics=semantics, vmem_limit_bytes=vmem_bytes)


def _norm_mod(x, gain, scale, shift):
    ms = jnp.mean(x * x, axis=-1, keepdims=True)
    return (x * lax.rsqrt(ms + EPS)) * gain * (1.0 + scale) + shift


def _norm_mod_tile(x_ref, gs_ref, sh_ref, h_ref, rows):
    x = x_ref[rows, :]
    ms = jnp.mean(x * x, axis=-1, keepdims=True)
    h_ref[rows, :] = ((x * lax.rsqrt(ms + EPS)) * gs_ref[...] + sh_ref[...]).astype(h_ref.dtype)


def _cast_plan(rows, cols, steps):
    options = []
    for size in range(BF16_ROWS, rows + 1, BF16_ROWS):
        if rows % size == 0 and rows // size <= steps:
            options.append((size * cols, 0, size, rows // size))
            break
    for size in range(V7X_LANES, cols + 1, V7X_LANES):
        if cols % size == 0 and cols // size <= steps:
            options.append((rows * size, 1, size, cols // size))
            break
    assert options, "weight cannot be cast in slabs within this grid"
    _, axis, size, count = min(options)
    return axis, size, count


def _side_casts(side_casts, steps, step_of):
    in_specs, out_specs, out_shapes, args, vmem = [], [], [], [], 0
    for stack, layer in side_casts:
        _, rows, cols = stack.shape
        axis, size, count = _cast_plan(rows, cols, steps)

        def slab(*ids, count=count):
            return jnp.minimum(step_of(*ids), count - 1)

        if axis == 0:
            in_specs.append(pl.BlockSpec((None, size, cols), lambda *ids, la=layer, sl=slab: (la, sl(*ids), 0)))
            out_specs.append(pl.BlockSpec((size, cols), lambda *ids, sl=slab: (sl(*ids), 0)))
            vmem += 2 * size * cols * 6
        else:
            in_specs.append(pl.BlockSpec((None, rows, size), lambda *ids, la=layer, sl=slab: (la, 0, sl(*ids))))
            out_specs.append(pl.BlockSpec((rows, size), lambda *ids, sl=slab: (0, sl(*ids))))
            vmem += 2 * rows * size * 6
        out_shapes.append(jax.ShapeDtypeStruct((rows, cols), BF16))
        args.append(stack)
    return in_specs, out_specs, out_shapes, args, vmem


def _cast_slabs(src_refs, dst_refs):
    for src, dst in zip(src_refs, dst_refs):
        dst[...] = src[...].astype(BF16)


def _silu(x):
    return x * jax.nn.sigmoid(x)


def _head_rms(o):
    return o * lax.rsqrt(jnp.mean(o * o, axis=-1, keepdims=True) + EPS)


def _mod_kernel(c_ref, w_ref, b_ref, o_ref):
    c_act = _silu(c_ref[...]).astype(BF16)
    o_ref[...] = jnp.dot(c_act, w_ref[...].astype(BF16), preferred_element_type=F32) + b_ref[...]


def _modulation(c, w_mod, b_mod):
    depth, d, n = w_mod.shape
    b = c.shape[0]
    tn = 1024
    return pl.pallas_call(
        _mod_kernel,
        grid=(depth, n // tn),
        in_specs=[
            pl.BlockSpec((b, d), lambda l, j: (0, 0)),
            pl.BlockSpec((None, d, tn), lambda l, j: (l, 0, j)),
            pl.BlockSpec((None, 1, tn), lambda l, j: (l, 0, j)),
        ],
        out_specs=pl.BlockSpec((None, b, tn), lambda l, j: (l, 0, j)),
        out_shape=jax.ShapeDtypeStruct((depth, b, n), F32),
        compiler_params=_params(("parallel", "parallel"), _vmem_limit(d * tn * 4, 0, d * tn * 2)),
        name="adaln_mod",
    )(c, w_mod, b_mod.reshape(depth, 1, n))


def _rope_kernel(pos_ref, invf_ref, *rest, n_casts):
    cast_srcs, (cos_ref, sin_ref), cast_dsts = rest[:n_casts], rest[n_casts:n_casts + 2], rest[n_casts + 2:]
    ang = pos_ref[...].astype(F32) * invf_ref[...]
    cos_ref[...] = jnp.cos(ang)
    sin_ref[...] = jnp.sin(ang)
    _cast_slabs(cast_srcs, cast_dsts)


def _rope_tables(positions, half, side_casts=()):
    b, s = positions.shape
    inv_freq = jnp.power(ROPE_BASE, -jnp.linspace(0.0, 1.0, half, dtype=F32)).reshape(1, half)
    ts = min(s, ROWS_ROPE)
    nj = s // ts
    spec = pl.BlockSpec((None, ts, half), lambda i, j: (i, j, 0))
    c_in, c_out, c_shape, c_args, c_vmem = _side_casts(side_casts, b * nj, lambda i, j: i * nj + j)
    return pl.pallas_call(
        functools.partial(_rope_kernel, n_casts=len(side_casts)),
        grid=(b, nj),
        in_specs=[pl.BlockSpec((None, ts, 1), lambda i, j: (i, j, 0)),
                  pl.BlockSpec((1, half), lambda i, j: (0, 0))] + c_in,
        out_specs=[spec, spec] + c_out,
        out_shape=[jax.ShapeDtypeStruct((b, s, half), F32)] * 2 + c_shape,
        compiler_params=_params(("arbitrary", "arbitrary"), _vmem_limit(3 * ts * V7X_LANES * 4, c_vmem, 8 * MIB)),
        name="rope_tables",
    )(positions.reshape(b, s, 1), inv_freq, *c_args)


def _rope_apply(t, cos, sin):
    half = cos.shape[-1]
    parts = []
    for c in range(0, t.shape[-1], 2 * half):
        t1, t2 = t[:, c:c + half], t[:, c + half:c + 2 * half]
        parts += [t1 * cos - t2 * sin, t2 * cos + t1 * sin]
    return jnp.concatenate(parts, axis=-1)


def _proj_in_kernel(*refs, col_steps, with_rope, with_z, n_casts):
    x_ref, sc_ref, sh_ref, g_ref, w_ref = refs[:5]
    rest = list(refs[5:])
    cos_ref, sin_ref = (rest.pop(0), rest.pop(0)) if with_rope else (None, None)
    wz_ref = rest.pop(0) if with_z else None
    cast_srcs = [rest.pop(0) for _ in range(n_casts)]
    o_ref = rest.pop(0)
    z_ref = rest.pop(0) if with_z else None
    cast_dsts = [rest.pop(0) for _ in range(n_casts)]
    h_ref, gs_ref = rest
    k = pl.program_id(2)

    def emit(posts, rows=slice(None)):
        y = jnp.dot(h_ref[rows, :], w_ref[...], preferred_element_type=F32)
        width = y.shape[1] // len(posts)
        for p, (kind, scale) in enumerate(posts):
            cols = slice(p * width, (p + 1) * width)
            part = y[:, cols]
            if kind == "rope":
                part = _rope_apply(part, cos_ref[rows, :], sin_ref[rows, :])
            elif kind == "silu":
                part = _silu(part)
            if scale != 1.0:
                part = part * scale
            o_ref[rows, cols] = part.astype(o_ref.dtype)

    def emit_step(posts):
        emit(posts)
        _cast_slabs(cast_srcs, cast_dsts)

    @pl.when(k == 0)
    def _():
        gs_ref[...] = g_ref[...] * (1.0 + sc_ref[...])
        tm = x_ref.shape[0]
        group = min(PROLOGUE_ROWS, tm)
        for r in range(0, tm, group):
            for t in range(r, r + group, BF16_ROWS):
                _norm_mod_tile(x_ref, gs_ref, sh_ref, h_ref, slice(t, t + BF16_ROWS))
            emit(col_steps[0][1], slice(r, r + group))
        if with_z:
            z_ref[...] = jnp.dot(h_ref[...], wz_ref[...], preferred_element_type=F32).astype(z_ref.dtype)
        _cast_slabs(cast_srcs, cast_dsts)

    start = 0
    for n_steps, posts in col_steps:
        pl.when(jnp.logical_and(k >= max(start, 1), k < start + n_steps))(functools.partial(emit_step, posts))
        start += n_steps


def _proj_in(x, scale, shift, gain, w, col_steps, rope=None, wz=None, side_casts=()):
    b, s, d = x.shape
    tm = min(s, ROWS_PROJ)
    tn = COLS_PROJ
    nj = s // tm
    nk = sum(n for n, _ in col_steps)
    n_cols = tn * nk
    with_z = wz is not None
    with_rope = rope is not None
    vec = pl.BlockSpec((None, 1, d), lambda i, j, k: (i, 0, 0))
    in_specs = [
        pl.BlockSpec((None, tm, d), lambda i, j, k: (i, j, 0)),
        vec,
        vec,
        pl.BlockSpec((1, d), lambda i, j, k: (0, 0)),
        pl.BlockSpec((d, tn), lambda i, j, k: (0, k)),
    ]
    out_specs = [pl.BlockSpec((None, tm, tn), lambda i, j, k: (i, j, k))]
    out_shape = [jax.ShapeDtypeStruct((b, s, n_cols), BF16)]
    args = [x, scale, shift, gain.reshape(1, d), w]
    if with_rope:
        half = rope[0].shape[-1]
        in_specs += [pl.BlockSpec((None, tm, half), lambda i, j, k: (i, j, 0))] * 2
        args += list(rope)
    if with_z:
        in_specs.append(pl.BlockSpec((d, V7X_LANES), lambda i, j, k: (0, 0)))
        out_specs.append(pl.BlockSpec((None, tm, V7X_LANES), lambda i, j, k: (i, j, 0)))
        out_shape.append(jax.ShapeDtypeStruct((b, s, V7X_LANES), BF16))
        args.append(wz)
    c_in, c_out, c_shape, c_args, c_vmem = _side_casts(side_casts, b * nj * nk, lambda i, j, k: (i * nj + j) * nk + k)
    outs = pl.pallas_call(
        functools.partial(_proj_in_kernel, col_steps=col_steps, with_rope=with_rope, with_z=with_z,
                          n_casts=len(side_casts)),
        grid=(b, nj, nk),
        in_specs=in_specs + c_in,
        out_specs=out_specs + c_out,
        out_shape=out_shape + c_shape,
        scratch_shapes=[pltpu.VMEM((tm, d), BF16), pltpu.VMEM((1, d), F32)],
        compiler_params=_params(
            ("arbitrary", "arbitrary", "arbitrary"),
            _vmem_limit(tm * d * 4 + d * tn * 2 + tm * tn * 2 + (d + 5 * tm) * V7X_LANES * 2, tm * d * 2 + c_vmem,
                        2 * tm * tn * 4),
        ),
        name="proj_in",
    )(*args, *c_args)
    return list(outs)


def _out_proj_kernel(a_ref, w_ref, x_ref, gate_ref, o_ref):
    y = jnp.dot(a_ref[...], w_ref[...], preferred_element_type=F32)
    o_ref[...] = x_ref[...] + gate_ref[...] * y


def _out_proj(a, w, x, gate):
    b, s, k = a.shape
    d = w.shape[1]
    tm = min(s, ROWS_OUT_PROJ if k > d else 2 * ROWS_OUT_PROJ)
    tile = pl.BlockSpec((None, tm, d), lambda i, j: (i, j, 0))
    return pl.pallas_call(
        _out_proj_kernel,
        grid=(b, s // tm),
        in_specs=[
            pl.BlockSpec((None, tm, k), lambda i, j: (i, j, 0)),
            pl.BlockSpec((k, d), lambda i, j: (0, 0), pipeline_mode=pl.Buffered(1)),
            tile,
            pl.BlockSpec((None, 1, d), lambda i, j: (i, 0, 0)),
        ],
        out_specs=tile,
        out_shape=jax.ShapeDtypeStruct((b, s, d), F32),
        compiler_params=_params(
            ("parallel", "parallel"),
            _vmem_limit(tm * k * 2 + 2 * tm * d * 4, k * d * 2, tm * d * 4),
        ),
        name="out_proj",
    )(a, w, x, gate)


def _ret_attn_kernel(lg_ref, q_ref, k_ref, v_ref, g_ref, o_ref, state_ref, *, chunk):
    @pl.when(pl.program_id(2) == 0)
    def _():
        state_ref[...] = jnp.zeros_like(state_ref)

    heads, dk, dv = state_ref.shape
    tile = q_ref.shape[0]
    ii = lax.broadcasted_iota(jnp.int32, (chunk, chunk), 0)
    jj = lax.broadcasted_iota(jnp.int32, (chunk, chunk), 1)
    r = lax.broadcasted_iota(jnp.int32, (chunk, 1), 0).astype(F32)
    sls = [pl.ds(c * chunk, chunk) for c in range(tile // chunk)]

    for h in range(heads):
        kc = slice(h * dk, (h + 1) * dk)
        vc = slice(h * dv, (h + 1) * dv)
        lg = lg_ref[h][:, 0:1]
        decay = jnp.where(ii >= jj, jnp.exp((ii - jj).astype(F32) * lg), 0.0)
        row_dec = jnp.exp((r + 1.0) * lg)
        col_dec = jnp.exp((chunk - 1.0 - r) * lg)
        chunk_dec = jnp.exp(chunk * lg)
        probs = []
        for sl in sls:
            scores = lax.dot_general(q_ref[sl, kc], k_ref[sl, kc], (((1,), (1,)), ((), ())),
                                     preferred_element_type=F32)
            probs.append((scores * decay).astype(BF16))
        outs = [jnp.dot(p, v_ref[sl, vc], preferred_element_type=F32) for p, sl in zip(probs, sls)]
        updates = [lax.dot_general((k_ref[sl, kc].astype(F32) * col_dec).astype(BF16), v_ref[sl, vc],
                                   (((0,), (0,)), ((), ())), preferred_element_type=F32) for sl in sls]
        state = state_ref[h]
        for c, sl in enumerate(sls):
            q_dec = (q_ref[sl, kc].astype(F32) * row_dec).astype(BF16)
            outs[c] = outs[c] + jnp.dot(q_dec, state.astype(BF16), preferred_element_type=F32)
            state = chunk_dec * state + updates[c]
        state_ref[h] = state
        for c, sl in enumerate(sls):
            o_ref[sl, vc] = (g_ref[sl, vc].astype(F32) * _head_rms(outs[c])).astype(o_ref.dtype)


def _ret_attn(qkvg, heads):
    b, s, n = qkvg.shape
    dk = n // (6 * heads)
    dv = 2 * dk
    tile = min(s, ROWS_RET)
    chunk = min(tile, RET_CHUNK)
    hs = RET_HEADS_PER_STEP
    groups = heads // hs
    log_gamma = jnp.log1p(-jnp.exp2(-5.0 - jnp.arange(heads, dtype=F32)))
    lg = jnp.broadcast_to(log_gamma[:, None, None], (heads, 1, V7X_LANES))
    return pl.pallas_call(
        functools.partial(_ret_attn_kernel, chunk=chunk),
        grid=(b, groups, s // tile),
        in_specs=[
            pl.BlockSpec((hs, 1, V7X_LANES), lambda i, h, j: (h, 0, 0)),
            pl.BlockSpec((None, tile, hs * dk), lambda i, h, j: (i, j, h)),
            pl.BlockSpec((None, tile, hs * dk), lambda i, h, j: (i, j, groups + h)),
            pl.BlockSpec((None, tile, hs * dv), lambda i, h, j: (i, j, groups + h)),
            pl.BlockSpec((None, tile, hs * dv), lambda i, h, j: (i, j, 2 * groups + h)),
        ],
        out_specs=pl.BlockSpec((None, tile, hs * dv), lambda i, h, j: (i, j, h)),
        out_shape=jax.ShapeDtypeStruct((b, s, heads * dv), BF16),
        scratch_shapes=[pltpu.VMEM((hs, dk, dv), F32)],
        compiler_params=_params(
            ("parallel", "parallel", "arbitrary"),
            _vmem_limit(tile * hs * (2 * dk + 3 * dv) * 2, hs * dk * dv * 4, 16 * MIB),
        ),
        name="retention_core",
    )(lg, qkvg, qkvg, qkvg, qkvg)


def _gla_attn_kernel(q_ref, k_ref, v_ref, g_ref, z_ref, wg_ref, bg_ref, o_ref, state_ref, *, chunk):
    @pl.when(pl.program_id(1) == 0)
    def _():
        state_ref[...] = jnp.zeros_like(state_ref)

    heads, dv, dk = state_ref.shape
    tile = q_ref.shape[0]
    shift = chunk.bit_length() - 1
    pre = jnp.dot(z_ref[...], wg_ref[...], preferred_element_type=F32) + bg_ref[...]
    log_a = (jnp.minimum(pre, 0.0) - jnp.log(1.0 + jnp.exp(-jnp.abs(pre)))) * (1.0 / GLA_GATE_TAU)
    ii = lax.broadcasted_iota(jnp.int32, (tile, tile), 0)
    jj = lax.broadcasted_iota(jnp.int32, (tile, tile), 1)
    same_chunk = (ii >> shift) == (jj >> shift)
    causal = jnp.logical_and(same_chunk, ii >= jj)
    sums = jnp.where(causal, 1.0, 0.0).astype(BF16)
    hi = log_a.astype(BF16)
    rest = log_a - hi.astype(F32)
    mid = rest.astype(BF16)
    lo = (rest - mid.astype(F32)).astype(BF16)
    b = (jnp.dot(sums, hi, preferred_element_type=F32) + jnp.dot(sums, mid, preferred_element_type=F32)
         + jnp.dot(sums, lo, preferred_element_type=F32))
    n_chunks = tile // chunk
    b_ends = [b[(c + 1) * chunk - 1:(c + 1) * chunk, :] for c in range(n_chunks)]
    b_last = jnp.concatenate([jnp.broadcast_to(e, (chunk, e.shape[1])) for e in b_ends], axis=0)

    k = k_ref[...].astype(F32)
    q_dec = (q_ref[...].astype(F32) * jnp.exp(b)).astype(BF16)
    k_dec = (k * jnp.exp(-b)).astype(BF16)
    k_carry = (k * jnp.exp(b_last - b)).astype(BF16)
    chunk_dec = [jnp.exp(e) for e in b_ends]

    kcs = [slice(h * dk, (h + 1) * dk) for h in range(heads)]
    vcs = [slice(h * dv, (h + 1) * dv) for h in range(heads)]
    probs = []
    for kc in kcs:
        scores = lax.dot_general(q_dec[:, kc], k_dec[:, kc], (((1,), (1,)), ((), ())), preferred_element_type=F32)
        probs.append(jnp.where(causal, scores, 0.0).astype(BF16))
    outs = [jnp.dot(p, v_ref[:, vc], preferred_element_type=F32) for p, vc in zip(probs, vcs)]
    states = [state_ref[h] for h in range(heads)]
    inter = [[] for _ in range(heads)]
    for c in range(tile // chunk):
        rows = slice(c * chunk, (c + 1) * chunk)
        for h, (kc, vc) in enumerate(zip(kcs, vcs)):
            inter[h].append(lax.dot_general(q_dec[rows, kc], states[h].astype(BF16), (((1,), (1,)), ((), ())),
                                            preferred_element_type=F32))
            update = lax.dot_general(v_ref[rows, vc], k_carry[rows, kc], (((0,), (0,)), ((), ())),
                                     preferred_element_type=F32)
            states[h] = chunk_dec[c][:, kc] * states[h] + update
    for h, vc in enumerate(vcs):
        state_ref[h] = states[h]
        o = outs[h] + jnp.concatenate(inter[h], axis=0)
        o_ref[:, vc] = (g_ref[:, vc].astype(F32) * _head_rms(o)).astype(o_ref.dtype)


def _gla_attn(proj, z, w_gate_up, b_gate, heads):
    b, s, n = proj.shape
    dk = n // (6 * heads)
    dv = 2 * dk
    tile = min(s, ROWS_GLA)
    return pl.pallas_call(
        functools.partial(_gla_attn_kernel, chunk=GLA_CHUNK),
        grid=(b, s // tile),
        in_specs=[
            pl.BlockSpec((None, tile, heads * dk), lambda i, j: (i, j, 0)),
            pl.BlockSpec((None, tile, heads * dk), lambda i, j: (i, j, 1)),
            pl.BlockSpec((None, tile, heads * dv), lambda i, j: (i, j, 1)),
            pl.BlockSpec((None, tile, heads * dv), lambda i, j: (i, j, 2)),
            pl.BlockSpec((None, tile, V7X_LANES), lambda i, j: (i, j, 0)),
            pl.BlockSpec((V7X_LANES, heads * dk), lambda i, j: (0, 0)),
            pl.BlockSpec((1, heads * dk), lambda i, j: (0, 0)),
        ],
        out_specs=pl.BlockSpec((None, tile, heads * dv), lambda i, j: (i, j, 0)),
        out_shape=jax.ShapeDtypeStruct((b, s, heads * dv), BF16),
        scratch_shapes=[pltpu.VMEM((heads, dv, dk), F32)],
        compiler_params=_params(
            ("parallel", "arbitrary"),
            _vmem_limit(tile * heads * (2 * dk + 3 * dv) * 2 + 2 * V7X_LANES * heads * dk * 2,
                        heads * dk * dv * 4, 24 * MIB),
        ),
        name="gla_core",
    )(proj, proj, proj, proj, z, w_gate_up, b_gate.reshape(1, -1))


def _conv_in_kernel(x_ref, sc_ref, sh_ref, g_ref, wb_ref, wc_ref, wu_ref, cw_ref, *rest, n_casts):
    cast_srcs, o_ref, cast_dsts = rest[:n_casts], rest[n_casts], rest[n_casts + 1:2 * n_casts + 1]
    h_ref, gs_ref, carry_ref = rest[2 * n_casts + 1:]
    j = pl.program_id(1)
    n = pl.program_id(2)
    tm, tn = o_ref.shape

    prev_tile = jnp.where(j > 0, carry_ref[n], 0.0)

    def conv_rows(r0, nrows, prev_rows):
        rows = slice(r0, r0 + nrows)
        h = h_ref[rows, :]
        row = lax.broadcasted_iota(jnp.int32, (nrows, 1), 0)
        last = {}
        for c in range(0, tn, 256):
            cols = slice(c, c + 256)
            b_gate = jnp.dot(h, wb_ref[:, cols], preferred_element_type=F32)
            c_gate = jnp.dot(h, wc_ref[:, cols], preferred_element_type=F32)
            u = c_gate * jnp.dot(h, wu_ref[:, cols], preferred_element_type=F32)
            prev = prev_rows[c]
            last[c] = u[nrows - V7X_SUBLANES:, :]
            u1 = jnp.where(row == 0, prev[7:8, :], pltpu.roll(u, 1, axis=0))
            u2 = jnp.where(row == 0, prev[6:7, :],
                           jnp.where(row == 1, prev[7:8, :], pltpu.roll(u, 2, axis=0)))
            y = cw_ref[0:1, cols] * u2 + cw_ref[1:2, cols] * u1 + cw_ref[2:3, cols] * u
            o_ref[rows, cols] = (b_gate * y).astype(o_ref.dtype)
        return last

    def store_carry(last):
        for c, rows8 in last.items():
            carry_ref[n, :, c:c + 256] = rows8

    tile_prev = {c: prev_tile[:, c:c + 256] for c in range(0, tn, 256)}

    @pl.when(n == 0)
    def _():
        gs_ref[...] = g_ref[...] * (1.0 + sc_ref[...])
        prev_rows = tile_prev
        group = min(PROLOGUE_ROWS, tm)
        for r in range(0, tm, group):
            for t in range(r, r + group, BF16_ROWS):
                _norm_mod_tile(x_ref, gs_ref, sh_ref, h_ref, slice(t, t + BF16_ROWS))
            prev_rows = conv_rows(r, group, prev_rows)
        store_carry(prev_rows)
        _cast_slabs(cast_srcs, cast_dsts)

    @pl.when(n > 0)
    def _():
        store_carry(conv_rows(0, tm, tile_prev))
        _cast_slabs(cast_srcs, cast_dsts)


def _conv_in(x, scale, shift, gain, w, conv_w, side_casts=()):
    b, s, d = x.shape
    tm = min(s, ROWS_CONV)
    tn = 512
    nb = d // tn
    nj = s // tm
    vec = pl.BlockSpec((None, 1, d), lambda i, j, n: (i, 0, 0))

    def w_spec(part):
        return pl.BlockSpec((d, tn), lambda i, j, n: (0, part * nb + n))

    c_in, c_out, c_shape, c_args, c_vmem = _side_casts(side_casts, b * nj * nb, lambda i, j, n: (i * nj + j) * nb + n)
    return pl.pallas_call(
        functools.partial(_conv_in_kernel, n_casts=len(side_casts)),
        grid=(b, nj, nb),
        in_specs=[
            pl.BlockSpec((None, tm, d), lambda i, j, n: (i, j, 0)),
            vec,
            vec,
            pl.BlockSpec((1, d), lambda i, j, n: (0, 0)),
            w_spec(0),
            w_spec(1),
            w_spec(2),
            pl.BlockSpec((CONV_WIDTH, tn), lambda i, j, n: (0, n)),
        ] + c_in,
        out_specs=[pl.BlockSpec((None, tm, tn), lambda i, j, n: (i, j, n))] + c_out,
        out_shape=[jax.ShapeDtypeStruct((b, s, d), BF16)] + c_shape,
        scratch_shapes=[pltpu.VMEM((tm, d), BF16), pltpu.VMEM((1, d), F32),
                        pltpu.VMEM((nb, V7X_SUBLANES, tn), F32)],
        compiler_params=_params(
            ("arbitrary", "arbitrary", "arbitrary"),
            _vmem_limit(tm * d * 4 + 3 * d * tn * 2 + tm * tn * 2, tm * d * 2 + c_vmem, 8 * tm * tn * 4),
        ),
        name="conv_in",
    )(x, scale, shift, gain.reshape(1, d), w, w, w, conv_w, *c_args)


def _pool_kernel(x_ref, sc_ref, sh_ref, g_ref, w_ref, ps_ref, gate_ref, o_ref, carry_ref):
    j = pl.program_id(1)
    tm, d = x_ref.shape
    halo = carry_ref.shape[0]
    group = d // len(POOL_WINDOWS)
    x = x_ref[...]
    h = _norm_mod(x, g_ref[...], sc_ref[...], sh_ref[...])
    prev = jnp.where(j > 0, carry_ref[...], 0.0)
    carry_ref[...] = h[tm - halo:, :]
    t = j * tm + lax.broadcasted_iota(jnp.int32, (tm, 1), 0)

    for gi, win in enumerate(POOL_WINDOWS):
        cols = slice(gi * group, (gi + 1) * group)
        hg = h[:, cols]
        ext = jnp.concatenate([prev[:, cols], hg], axis=0)
        k = 1
        while k < win:
            ext = ext + pltpu.roll(ext, k, axis=0)
            k *= 2
        inv_count = 1.0 / jnp.minimum(t + 1, win).astype(F32)
        mixed = ext[halo:, :] * inv_count - hg
        y = jnp.dot(mixed.astype(BF16), w_ref[gi], preferred_element_type=F32)
        o_ref[:, cols] = x[:, cols] + gate_ref[:, cols] * (y * ps_ref[:, cols])


def _pool_mixer(x, scale, shift, gain, pool_w, pool_scale, gate):
    b, s, d = x.shape
    groups, p, _ = pool_w.shape
    tm = min(s, ROWS_POOL)
    halo = 2 * V7X_SUBLANES
    vec = pl.BlockSpec((None, 1, d), lambda i, j: (i, 0, 0))
    row = pl.BlockSpec((1, d), lambda i, j: (0, 0))
    tile = pl.BlockSpec((None, tm, d), lambda i, j: (i, j, 0))
    return pl.pallas_call(
        _pool_kernel,
        grid=(b, s // tm),
        in_specs=[tile, vec, vec, row, pl.BlockSpec((groups, p, p), lambda i, j: (0, 0, 0)), row, vec],
        out_specs=tile,
        out_shape=jax.ShapeDtypeStruct((b, s, d), F32),
        scratch_shapes=[pltpu.VMEM((halo, d), F32)],
        compiler_params=_params(
            ("parallel", "arbitrary"),
            _vmem_limit(2 * tm * d * 4 + groups * p * p * 2, halo * d * 4, 6 * tm * d * 4),
        ),
        name="pool_mixer",
    )(x, scale, shift, gain.reshape(1, d), pool_w, pool_scale.reshape(1, d), gate)


def _ffn_kernel(x_ref, mod_ref, gains_ref, wa_ref, wb_ref, wo_ref, *rest, final_norm, n_casts):
    sh_ref, sc_ref, gate_ref = (mod_ref.at[p:p + 1] for p in range(3))
    g_ref, fg_ref = gains_ref.at[0:1], gains_ref.at[1:2]
    cast_srcs, o_ref, cast_dsts = rest[:n_casts], rest[n_casts], rest[n_casts + 1:2 * n_casts + 1]
    h_ref, gs_ref = rest[2 * n_casts + 1:]
    f = pl.program_id(2)

    def ffn_part(rows):
        h = h_ref[rows, :]
        acts = []
        for c in range(0, wa_ref.shape[1], 256):
            a = jnp.dot(h, wa_ref[:, c:c + 256], preferred_element_type=F32)
            b = jnp.dot(h, wb_ref[:, c:c + 256], preferred_element_type=F32)
            acts.append((_silu(a) * b).astype(BF16))
        act = jnp.concatenate(acts, axis=-1)
        return gate_ref[...] * jnp.dot(act, wo_ref[...], preferred_element_type=F32)

    @pl.when(f == 0)
    def _():
        gs_ref[...] = g_ref[...] * (1.0 + sc_ref[...])
        group = min(PROLOGUE_ROWS, x_ref.shape[0])
        for r in range(0, x_ref.shape[0], group):
            rows = slice(r, r + group)
            for t in range(r, r + group, BF16_ROWS):
                _norm_mod_tile(x_ref, gs_ref, sh_ref, h_ref, slice(t, t + BF16_ROWS))
            o_ref[rows, :] = x_ref[rows, :] + ffn_part(rows)
        _cast_slabs(cast_srcs, cast_dsts)

    @pl.when(f > 0)
    def _():
        o_ref[...] += ffn_part(slice(None))
        _cast_slabs(cast_srcs, cast_dsts)

    if final_norm:
        @pl.when(f == pl.num_programs(2) - 1)
        def _():
            for r in range(0, o_ref.shape[0], 4 * V7X_SUBLANES):
                rows = slice(r, r + 4 * V7X_SUBLANES)
                o_ref[rows, :] = _head_rms(o_ref[rows, :]) * fg_ref[...]


def _ffn(x, mods, gain, w_in, w_out, final_gain, final_norm, side_casts=()):
    b, s, d = x.shape
    dff = w_out.shape[0]
    tm = min(s, ROWS_FFN)
    tf = 512
    nf = dff // tf
    nj = s // tm
    tile = pl.BlockSpec((None, tm, d), lambda i, j, f: (i, j, 0))
    c_in, c_out, c_shape, c_args, c_vmem = _side_casts(side_casts, b * nj * nf, lambda i, j, f: (i * nj + j) * nf + f)
    return pl.pallas_call(
        functools.partial(_ffn_kernel, final_norm=final_norm, n_casts=len(side_casts)),
        grid=(b, nj, nf),
        in_specs=[
            tile,
            pl.BlockSpec((None, 3, d), lambda i, j, f: (i, 0, 0)),
            pl.BlockSpec((2, d), lambda i, j, f: (0, 0)),
            pl.BlockSpec((d, tf), lambda i, j, f: (0, f)),
            pl.BlockSpec((d, tf), lambda i, j, f: (0, nf + f)),
            pl.BlockSpec((tf, d), lambda i, j, f: (f, 0)),
        ] + c_in,
        out_specs=[tile] + c_out,
        out_shape=[jax.ShapeDtypeStruct((b, s, d), F32)] + c_shape,
        scratch_shapes=[pltpu.VMEM((tm, d), BF16), pltpu.VMEM((1, d), F32)],
        compiler_params=_params(
            ("arbitrary", "arbitrary", "arbitrary"),
            _vmem_limit(2 * tm * d * 4 + 3 * d * tf * 2, tm * d * 2 + c_vmem, 4 * tm * 256 * 4),
        ),
        name="swiglu_ffn",
    )(x, mods, jnp.stack([gain, final_gain]), w_in, w_in, w_out, *c_args)


def kernel(x, c, positions, w_mod, b_mod, norm1_g, norm2_g, ret_w_in, ret_w_out, conv_w_in, conv_w, conv_w_out,
           gla_w_in, gla_w_gate_up, gla_b_gate, gla_w_out, pool_w, pool_scale, ffn_w_in, ffn_w_out, final_g):
    b, s, d = x.shape
    depth = w_mod.shape[0]
    assert depth >= 1 and d % (len(POOL_WINDOWS) * V7X_LANES) == 0 and s % GLA_CHUNK == 0

    mod = _modulation(c, w_mod, b_mod).reshape(depth, b, 6, 1, d)

    mixer_weights = {0: {"w_in": ret_w_in, "w_out": ret_w_out}, 1: {"w_in": conv_w_in, "w_out": conv_w_out},
                     2: {"w_in": gla_w_in, "w_out": gla_w_out}, 3: {}}
    ready = {}

    def pending(layer):
        return [((layer, name), stack, layer // N_MIXERS) for name, stack in mixer_weights[layer % N_MIXERS].items()]

    def hostable(todo):
        return [t for t in todo if t[1].shape[-1] % V7X_LANES == 0]

    def take(key, stack, index):
        return ready.pop(key) if key in ready else stack[index].astype(BF16)

    def host(call, todo):
        outs = call(side_casts=tuple((stack, index) for _, stack, index in todo))
        for (key, _, _), w in zip(todo, outs[len(outs) - len(todo):]):
            ready[key] = w
        return outs[:len(outs) - len(todo)]

    ret_dk = d // RET_HEADS
    cos, sin = host(functools.partial(_rope_tables, positions, ret_dk // 2), hostable(pending(0)))
    lane_pad = V7X_LANES - GLA_GATE_RANK
    for i in range(depth):
        sh1, sc1, g1 = (mod[i, :, p] for p in range(3))
        m, j = i % N_MIXERS, i // N_MIXERS
        w_in, w_out = (take((i, name), stack, j) for (_, name), stack, j in pending(i)) if m < 3 else (None, None)
        todo = hostable(pending(i + 1)) if i + 1 < depth else []
        if i == 0:
            todo = todo + [((0, "ffn_in"), ffn_w_in, 0), ((0, "ffn_out"), ffn_w_out, 0)]
        if m == 0:
            assert w_in.shape[1] == 6 * COLS_PROJ
            col_steps = ((1, (("rope", 1.0),)), (1, (("rope", ret_dk ** -0.5),)),
                         (2, (("plain", 1.0),)), (2, (("silu", 1.0),)))
            (qkvg,) = host(functools.partial(_proj_in, x, sc1, sh1, norm1_g[i], w_in, col_steps, rope=(cos, sin)),
                           todo)
            x = _out_proj(_ret_attn(qkvg, RET_HEADS), w_out, x, g1)
        elif m == 1:
            (z,) = host(functools.partial(_conv_in, x, sc1, sh1, norm1_g[i], w_in, conv_w[j]), todo)
            x = _out_proj(z, w_out, x, g1)
        elif m == 2:
            gla_cols = w_in.shape[1] - GLA_GATE_RANK
            wz = jnp.pad(w_in[:, gla_cols:], ((0, 0), (0, lane_pad)))
            w_up = jnp.pad(gla_w_gate_up[j], ((0, lane_pad), (0, 0))).astype(BF16)
            assert gla_cols == 3 * COLS_PROJ
            col_steps = ((1, (("plain", (gla_cols // 6 // GLA_HEADS) ** -0.5), ("plain", 1.0))),
                         (1, (("plain", 1.0),)), (1, (("silu", 1.0),)))
            proj, z = host(functools.partial(_proj_in, x, sc1, sh1, norm1_g[i], w_in, col_steps, wz=wz), todo)
            x = _out_proj(_gla_attn(proj, z, w_up, gla_b_gate[j], GLA_HEADS), w_out, x, g1)
        else:
            x = _pool_mixer(x, sc1, sh1, norm1_g[i], pool_w[j].astype(BF16), pool_scale[j], g1)
            for key, stack, index in todo:
                ready[key] = stack[index].astype(BF16)
        ffn_in, ffn_out = take((i, "ffn_in"), ffn_w_in, i), take((i, "ffn_out"), ffn_w_out, i)
        todo = [((i + 1, "ffn_in"), ffn_w_in, i + 1), ((i + 1, "ffn_out"), ffn_w_out, i + 1)] if i + 1 < depth else []
        (x,) = host(functools.partial(_ffn, x, mod[i, :, 3:6, 0], norm2_g[i], ffn_in, ffn_out, final_g,
                                      i == depth - 1), todo)
    return x
```
